```python
import math
import jax
import jax.numpy as jnp
from jax import lax
import numpy as np

D_MODEL = 1024
BATCH = 2
SEQ = 8192
DEPTH = 2

CTX_LEN = 256
GRID_W = 64
BR_W = D_MODEL // 2
N_BRANCH = 3
EPS = 1e-6
DA_DH = 64
DA_DV = 2 * DA_DH
DA_HEADS = BR_W // DA_DV
ROPE_BASE = 10000.0
Q_BLOCK = 128
ML_DH = 128
ML_HEADS = BR_W // ML_DH
ML_CHUNK = 64
ML_CONV = 3
NA_DH = 64
NA_HEADS = BR_W // NA_DH
NA_KH_MAX = 8
NA_KW = 16
NA_QB = 16
NA_KBW = NA_QB + NA_KW
PROJ_SIZES = (BR_W,) * 4 + (BR_W,) * 5 + (4 * ML_HEADS,) + (BR_W,) * 4 + (N_BRANCH * D_MODEL,)
D_IN = sum(PROJ_SIZES)

kernel_name = 'hybrid_diffattn_mlstm_natten_prefix_block'


def rms_norm(x, g):
    xf = x.astype(jnp.float32)
    y = xf * lax.rsqrt(jnp.mean(xf * xf, axis=-1, keepdims=True) + EPS)
    return (y * g).astype(x.dtype)


def split_proj(p):
    idx = [int(i) for i in np.cumsum(PROJ_SIZES)[:-1]]
    return jnp.split(p, idx, axis=-1)


def axial_rope_tables(n_tok, dh):
    t = jnp.arange(n_tok, dtype=jnp.int32)
    row = (t // GRID_W).astype(jnp.float32)
    col = (t % GRID_W).astype(jnp.float32)
    inv = ROPE_BASE ** (-jnp.arange(0, dh // 2, 2, dtype=jnp.float32) / (dh // 2))
    ang = jnp.concatenate([row[:, None] * inv, col[:, None] * inv], axis=-1)
    return jnp.cos(ang), jnp.sin(ang)


def apply_rope(x, cos, sin):
    xp = x.reshape(x.shape[:-1] + (x.shape[-1] // 2, 2))
    x1, x2 = xp[..., 0], xp[..., 1]
    shp = (1, cos.shape[0]) + (1,) * (x.ndim - 3) + (cos.shape[1],)
    cs = cos.reshape(shp).astype(x.dtype)
    sn = sin.reshape(shp).astype(x.dtype)
    return jnp.stack([x1 * cs - x2 * sn, x1 * sn + x2 * cs], axis=-1).reshape(x.shape)


def softmax_attn(q, k, v):
    s = jnp.einsum('bqhd,bkhd->bhqk', q, k).astype(jnp.float32) * (q.shape[-1] ** -0.5)
    p = jax.nn.softmax(s, axis=-1).astype(v.dtype)
    return jnp.einsum('bhqk,bkhd->bqhd', p, v)


def diff_attn_dense(q, k, v, lam):
    s = jnp.einsum('bqhcd,bkhcd->bhcqk', q, k).astype(jnp.float32) * (q.shape[-1] ** -0.5)
    p = jax.nn.softmax(s, axis=-1)
    a = (p[:, :, 0] - lam * p[:, :, 1]).astype(v.dtype)
    return jnp.einsum('bhqk,bkhv->bqhv', a, v)


def diff_attn_blocks(q, k, v, lam):
    B, S = q.shape[:2]
    nb = S // Q_BLOCK
    qb = jnp.moveaxis(q.reshape((B, nb, Q_BLOCK) + q.shape[2:]), 1, 0)
    o = lax.map(lambda qq: diff_attn_dense(qq, k, v, lam), qb)
    return jnp.moveaxis(o, 0, 1).reshape((B, S) + o.shape[3:])


def diff_head_norm(o, g, lam_init):
    y = rms_norm(o, g) * (1.0 - lam_init)
    return y.reshape(o.shape[:2] + (-1,))


def dwconv_centred(x, w, b):
    K = w.shape[0]
    y = lax.conv_general_dilated(x, w[:, None, :].astype(x.dtype), window_strides=(1,),
                                 padding=[((K - 1) // 2, K // 2)],
                                 dimension_numbers=('NWC', 'WIO', 'NWC'),
                                 feature_group_count=x.shape[-1])
    return y + b


def mlstm_inputs(q, k, v, g, conv_w, conv_b, gate_b):
    B, T, _ = q.shape
    qk = jax.nn.silu(dwconv_centred(jnp.concatenate([q, k], axis=-1), conv_w, conv_b))
    qq, kk = jnp.split(qk, 2, axis=-1)
    shp = (B, T, ML_HEADS, ML_DH)
    gates = g.reshape(B, T, 2, 2, ML_HEADS).astype(jnp.float32) + gate_b.astype(jnp.float32)
    log_i = gates[:, :, :, 0]
    log_f = jax.nn.log_sigmoid(gates[:, :, :, 1])
    return qq.reshape(shp), kk.reshape(shp) * (ML_DH ** -0.5), v.reshape(shp), log_i, log_f


def mlstm_chunkwise(q, k, v, log_i, log_f, state, with_out):
    B, T, H, d = q.shape
    L = ML_CHUNK
    nc = T // L
    to_chunks = lambda a: jnp.moveaxis(a.reshape((B, nc, L) + a.shape[2:]), 1, 0)
    xs = (to_chunks(q), to_chunks(k), to_chunks(v), to_chunks(log_i), to_chunks(log_f))
    causal = jnp.tril(jnp.ones((L, L), dtype=bool))

    def step(carry, xc):
        C, n, m = carry
        qc, kc, vc, lic, lfc = xc
        qc = qc.astype(jnp.float32)
        kc = kc.astype(jnp.float32)
        vc = vc.astype(jnp.float32)
        b = jnp.moveaxis(jnp.cumsum(lfc.astype(jnp.float32), axis=1), 1, 2)
        li = jnp.moveaxis(lic.astype(jnp.float32), 1, 2)
        bL = b[..., -1]
        g_s = bL[..., None] - b + li
        m_new = jnp.maximum(bL + m, jnp.max(g_s, axis=-1))
        w_s = jnp.exp(g_s - m_new[..., None])
        decay = jnp.exp(bL + m - m_new)
        C_new = decay[..., None, None] * C + jnp.einsum('bhs,bshk,bshv->bhkv', w_s, kc, vc)
        n_new = decay[..., None] * n + jnp.einsum('bhs,bshk->bhk', w_s, kc)
        if with_out:
            dmat = jnp.where(causal, b[..., :, None] - b[..., None, :] + li[..., None, :], -jnp.inf)
            m_inter = b + m[..., None]
            m_t = jnp.maximum(m_inter, jnp.max(dmat, axis=-1))
            a_inter = jnp.exp(m_inter - m_t)
            qk = jnp.einsum('bthd,bshd->bhts', qc, kc) * jnp.exp(dmat - m_t[..., None])
            num = a_inter[..., None] * jnp.einsum('bthk,bhkv->bhtv', qc, C) + jnp.einsum('bhts,bshv->bhtv', qk, vc)
            den = a_inter * jnp.einsum('bthk,bhk->bht', qc, n) + jnp.sum(qk, axis=-1)
            h = num / jnp.maximum(jnp.abs(den), jnp.exp(-m_t))[..., None]
            h = jnp.moveaxis(h, 1, 2)
        else:
            h = None
        return (C_new, n_new, m_new), h

    final, hs = lax.scan(step, state, xs)
    if with_out:
        hs = jnp.moveaxis(hs, 0, 1).reshape(B, T, H, d).astype(q.dtype)
    return final, hs


def mlstm_bidirectional(lat, ctx, with_ctx_out):
    ql, kl, vl, lil, lfl = lat
    qc, kc, vc, lic, lfc = ctx
    B, _, H, d = ql.shape
    zero = (jnp.zeros((B, H, d, d), jnp.float32), jnp.zeros((B, H, d), jnp.float32), jnp.zeros((B, H), jnp.float32))
    flip = lambda a: jnp.flip(a, axis=1)
    st_f, hc_f = mlstm_chunkwise(qc, kc, vc, lic[:, :, 0], lfc[:, :, 0], zero, with_ctx_out)
    _, hl_f = mlstm_chunkwise(ql, kl, vl, lil[:, :, 0], lfl[:, :, 0], st_f, True)
    st_b, hc_b = mlstm_chunkwise(flip(qc), flip(kc), flip(vc), flip(lic[:, :, 1]), flip(lfc[:, :, 1]), zero, with_ctx_out)
    _, hl_b = mlstm_chunkwise(flip(ql), flip(kl), flip(vl), flip(lil[:, :, 1]), flip(lfl[:, :, 1]), st_b, True)
    h_lat = hl_f + flip(hl_b)
    h_ctx = hc_f + flip(hc_b) if with_ctx_out else None
    return h_lat, h_ctx


def mlstm_out(h, o_pre, z, g):
    hn = rms_norm(h, g).reshape(h.shape[:2] + (-1,))
    return jax.nn.sigmoid(o_pre) * hn * jax.nn.silu(z)


def neighbourhood_attn(q, k, v, kc, vc, rpb):
    B, S, H, d = q.shape
    rows = S // GRID_W
    kh = min(NA_KH_MAX, rows)
    ncb = GRID_W // NA_QB
    qcol = np.arange(GRID_W).reshape(ncb, NA_QB)
    kc0 = np.clip(np.arange(ncb) * NA_QB - NA_KW // 2, 0, GRID_W - NA_KBW)
    kcol = kc0[:, None] + np.arange(NA_KBW)[None, :]
    cs = np.clip(qcol - NA_KW // 2, 0, GRID_W - NA_KW)
    col_ok = (kcol[:, None, :] >= cs[:, :, None]) & (kcol[:, None, :] < cs[:, :, None] + NA_KW)
    col_idx = np.clip(kcol[:, None, :] - qcol[:, :, None] + NA_KW - 1, 0, 2 * NA_KW - 2)
    mask = jnp.asarray(col_ok[:, :, None, :])
    qb = q.reshape(B, rows, ncb, NA_QB, H, d)
    kb = k.reshape(B, rows, GRID_W, H, d)[:, :, kcol]
    vb = v.reshape(B, rows, GRID_W, H, d)[:, :, kcol]
    scale = d ** -0.5
    n_loc = kh * NA_KBW

    def row_block(r):
        rs = jnp.clip(r - kh // 2, 0, rows - kh)
        kr = lax.dynamic_slice_in_dim(kb, rs, kh, axis=1)
        vr = lax.dynamic_slice_in_dim(vb, rs, kh, axis=1)
        qr = lax.dynamic_index_in_dim(qb, r, axis=1, keepdims=False)
        row_idx = rs + jnp.arange(kh) - r + NA_KH_MAX - 1
        bias = rpb[:, row_idx[None, None, :, None], col_idx[:, :, None, :]]
        s_loc = jnp.einsum('bnqhd,binkhd->bhnqik', qr, kr).astype(jnp.float32) * scale + bias.astype(jnp.float32)
        s_loc = jnp.where(mask, s_loc, -jnp.inf)
        s_ctx = jnp.einsum('bnqhd,bjhd->bhnqj', qr, kc).astype(jnp.float32) * scale
        s_all = jnp.concatenate([s_loc.reshape(s_loc.shape[:4] + (n_loc,)), s_ctx], axis=-1)
        p = jax.nn.softmax(s_all, axis=-1).astype(v.dtype)
        p_loc = p[..., :n_loc].reshape(s_loc.shape)
        p_ctx = p[..., n_loc:]
        return jnp.einsum('bhnqik,binkhd->bnqhd', p_loc, vr) + jnp.einsum('bhnqj,bjhd->bnqhd', p_ctx, vc)

    o = lax.map(row_block, jnp.arange(rows))
    return jnp.moveaxis(o, 0, 1).reshape(B, S, H, d)


def merge_branches(ys, gm, w_br, w_out):
    y = jnp.stack(ys, axis=2)
    proj = jnp.einsum('btnw,nwd->btnd', y, w_br)
    g = jax.nn.sigmoid(gm.reshape(gm.shape[:2] + (N_BRANCH, -1)))
    return jnp.sum(g * proj, axis=2) @ w_out


def hybrid_layer(l, x, xc, sc, scc, cos, sin, w_mod, b_mod, norm_g, w_in, lam_q, lam_k, da_g,
                 conv_w, conv_b, gate_b, ml_g, rpb, w_br, w_out, update_ctx):
    B, S, _ = x.shape
    L = xc.shape[1]
    shift, scale, gate = jnp.split(sc @ w_mod + b_mod, 3, axis=-1)
    shift_c, scale_c, gate_c = jnp.split(scc @ w_mod + b_mod, 3, axis=-1)
    h = rms_norm(x, norm_g) * (1 + scale[:, None]) + shift[:, None]
    hc = rms_norm(xc, norm_g) * (1 + scale_c) + shift_c
    aq, ak, av, az, bq, bk, bv, bo, bz, bg, cq, ck, cv, cz, gm = split_proj(h @ w_in)
    aqc, akc, avc, azc, bqc, bkc, bvc, boc, bzc, bgc, cqc, ckc, cvc, czc, gmc = split_proj(hc @ w_in)

    lam_init = 0.8 - 0.6 * math.exp(-0.3 * l)
    lam = (jnp.exp(jnp.sum(lam_q[0] * lam_k[0])) - jnp.exp(jnp.sum(lam_q[1] * lam_k[1]))).astype(jnp.float32) + lam_init
    da_qk = lambda t: t.reshape(t.shape[:2] + (DA_HEADS, 2, DA_DH))
    da_v = lambda t: t.reshape(t.shape[:2] + (DA_HEADS, DA_DV))
    k_all = jnp.concatenate([apply_rope(da_qk(ak), cos, sin), da_qk(akc)], axis=1)
    v_all = jnp.concatenate([da_v(av), da_v(avc)], axis=1)
    o_a = diff_attn_blocks(apply_rope(da_qk(aq), cos, sin), k_all, v_all, lam)
    y_a = diff_head_norm(o_a, da_g, lam_init) * jax.nn.silu(az)

    lat_b = mlstm_inputs(bq, bk, bv, bg, conv_w, conv_b, gate_b)
    ctx_b = mlstm_inputs(bqc, bkc, bvc, bgc, conv_w, conv_b, gate_b)
    h_b, h_bc = mlstm_bidirectional(lat_b, ctx_b, update_ctx)
    y_b = mlstm_out(h_b, bo, bz, ml_g)

    na_h = lambda t: t.reshape(t.shape[:2] + (NA_HEADS, NA_DH))
    o_c = neighbourhood_attn(na_h(cq), na_h(ck), na_h(cv), na_h(ckc), na_h(cvc), rpb)
    y_c = o_c.reshape(B, S, BR_W) * jax.nn.silu(cz)

    x = x + gate[:, None] * merge_branches((y_a, y_b, y_c), gm, w_br, w_out)
    if update_ctx:
        o_ac = diff_attn_dense(da_qk(aqc), da_qk(akc), da_v(avc), lam)
        y_ac = diff_head_norm(o_ac, da_g, lam_init) * jax.nn.silu(azc)
        y_bc = mlstm_out(h_bc, boc, bzc, ml_g)
        y_cc = softmax_attn(na_h(cqc), na_h(ckc), na_h(cvc)).reshape(B, L, BR_W) * jax.nn.silu(czc)
        xc = xc + gate_c * merge_branches((y_ac, y_bc, y_cc), gmc, w_br, w_out)
    return x, xc


def setup_inputs(seed: int = 0) -> dict:
    key = jax.random.key(seed)
    ks = jax.random.split(key, 20)
    f32 = jnp.float32
    nrm = lambda k, shape, s: jax.random.normal(k, shape, f32) * s
    f_base = jnp.linspace(3.0, 6.0, ML_HEADS, dtype=f32)
    gate_base = jnp.stack([jnp.zeros((ML_HEADS,), f32), f_base])[None, None]
    return {
        'x': nrm(ks[0], (BATCH, SEQ, D_MODEL), 1.0),
        'c': nrm(ks[1], (BATCH, D_MODEL), 1.0),
        'ctx': nrm(ks[2], (BATCH, CTX_LEN, D_MODEL), 1.0),
        'c_ctx': nrm(ks[3], (D_MODEL,), 1.0),
        'w_mod': nrm(ks[4], (DEPTH, D_MODEL, 3 * D_MODEL), 0.5 * D_MODEL ** -0.5),
        'b_mod': nrm(ks[5], (DEPTH, 3 * D_MODEL), 0.02),
        'norm_g': 1.0 + nrm(ks[6], (DEPTH, D_MODEL), 0.02),
        'w_in': nrm(ks[7], (DEPTH, D_MODEL, D_IN), D_MODEL ** -0.5),
        'da_lam_q': nrm(ks[8], (DEPTH, 2, DA_DH), 0.1),
        'da_lam_k': nrm(ks[9], (DEPTH, 2, DA_DH), 0.1),
        'da_norm_g': 1.0 + nrm(ks[10], (DEPTH, DA_DV), 0.02),
        'ml_conv_w': nrm(ks[11], (DEPTH, ML_CONV, 2 * BR_W), ML_CONV ** -0.5),
        'ml_conv_b': nrm(ks[12], (DEPTH, 2 * BR_W), 0.02),
        'ml_gate_b': gate_base + nrm(ks[13], (DEPTH, 2, 2, ML_HEADS), 0.1),
        'ml_norm_g': 1.0 + nrm(ks[14], (DEPTH, ML_DH), 0.02),
        'na_rpb': nrm(ks[15], (DEPTH, NA_HEADS, 2 * NA_KH_MAX - 1, 2 * NA_KW - 1), 0.05),
        'w_br': nrm(ks[16], (DEPTH, N_BRANCH, BR_W, D_MODEL), BR_W ** -0.5),
        'w_out': nrm(ks[17], (DEPTH, D_MODEL, D_MODEL), D_MODEL ** -0.5),
        'final_g': 1.0 + nrm(ks[18], (D_MODEL,), 0.02),
    }


def reference(x, c, ctx, c_ctx, w_mod, b_mod, norm_g, w_in, da_lam_q, da_lam_k, da_norm_g,
              ml_conv_w, ml_conv_b, ml_gate_b, ml_norm_g, na_rpb, w_br, w_out, final_g):
    cos, sin = axial_rope_tables(x.shape[1], DA_DH)
    sc = jax.nn.silu(c)
    scc = jax.nn.silu(c_ctx)
    xc = ctx
    for l in range(DEPTH):
        x, xc = hybrid_layer(l, x, xc, sc, scc, cos, sin, w_mod[l], b_mod[l], norm_g[l], w_in[l],
                             da_lam_q[l], da_lam_k[l], da_norm_g[l], ml_conv_w[l], ml_conv_b[l],
                             ml_gate_b[l], ml_norm_g[l], na_rpb[l], w_br[l], w_out[l],
                             l < DEPTH - 1)
    return rms_norm(x, final_g)
```

```python
import functools
import math

import numpy as np
import jax
import jax.numpy as jnp
from jax import lax
from jax.experimental import pallas as pl
from jax.experimental.pallas import tpu as pltpu

F32 = jnp.float32
BF16 = jnp.bfloat16

D_MODEL = 1024
BR_W = 512
GRID_W = 64
EPS = 1e-6
ROPE_BASE = 10000.0
DA_HEADS, DA_DH, DA_DV = 4, 64, 128
ML_HEADS, ML_DH, ML_CHUNK = 4, 128, 64
NA_HEADS, NA_DH, NA_KH, NA_KW = 8, 64, 8, 16
N_GATE = 4 * ML_HEADS
GATE_PAD = 512
P_W = 9 * BR_W + GATE_PAD + 4 * BR_W + 3 * D_MODEL
U_AQ, U_AK, U_AV, U_AZ, U_BQ, U_BK, U_BV, U_BO, U_BZ, U_BG, U_CQ, U_CK, U_CV, U_CZ, U_GM = range(15)
NEG = -1e30
VMEM_LIMIT = 56 * 1024 * 1024
ROW_TILE = 768


def _cparams(*sem):
    return pltpu.CompilerParams(dimension_semantics=sem, vmem_limit_bytes=VMEM_LIMIT)


def _silu(x):
    return x * jax.nn.sigmoid(x)


def _mod_kernel(c_ref, w_ref, b_ref, o_ref):
    s = _silu(c_ref[...])
    o_ref[0] = jnp.dot(s.astype(BF16), w_ref[0].astype(BF16), preferred_element_type=F32) + b_ref[0]


def _modulation(cs, w_mod, b_mod):
    depth, d, d3 = w_mod.shape
    tn = 1024
    return pl.pallas_call(
        _mod_kernel,
        grid=(depth, d3 // tn),
        in_specs=[pl.BlockSpec((8, d), lambda l, j: (0, 0)),
                  pl.BlockSpec((1, d, tn), lambda l, j: (l, 0, j)),
                  pl.BlockSpec((1, 1, tn), lambda l, j: (l, 0, j))],
        out_specs=pl.BlockSpec((1, 8, tn), lambda l, j: (l, 0, j)),
        out_shape=jax.ShapeDtypeStruct((depth, 8, d3), F32),
        compiler_params=_cparams("parallel", "parallel"),
        name="modulation",
    )(cs, w_mod, b_mod.reshape(depth, 1, d3))


def _proj_kernel(x_ref, mod_ref, g_ref, w_ref, o_ref, h_scr, *, n_lat, n_batch, tm):
    b = pl.program_id(0)
    i = pl.program_id(1)
    j = pl.program_id(2)

    @pl.when(j == 0)
    def _():
        x = x_ref[0]
        ms = jnp.mean(x * x, axis=-1, keepdims=True)
        y = x * lax.rsqrt(ms + EPS) * g_ref[...]
        row = i * tm + lax.broadcasted_iota(jnp.int32, (tm, 1), 0)
        is_ctx = row >= n_lat
        mb = mod_ref[pl.ds(b, 1), :]
        mc = mod_ref[n_batch:n_batch + 1, :]
        shift = jnp.where(is_ctx, mc[:, :D_MODEL], mb[:, :D_MODEL])
        scale = jnp.where(is_ctx, mc[:, D_MODEL:2 * D_MODEL], mb[:, D_MODEL:2 * D_MODEL])
        h_scr[...] = (y * (1.0 + scale) + shift).astype(BF16)

    o_ref[0] = jnp.dot(h_scr[...], w_ref[...], preferred_element_type=F32)


def _projection(xa, mod, norm_g, w_pad, n_lat):
    nb, nt, d = xa.shape
    tm, tn = ROW_TILE, 1024
    return pl.pallas_call(
        functools.partial(_proj_kernel, n_lat=n_lat, n_batch=nb, tm=tm),
        grid=(nb, nt // tm, P_W // tn),
        in_specs=[pl.BlockSpec((1, tm, d), lambda b, i, j: (b, i, 0)),
                  pl.BlockSpec((8, 3 * d), lambda b, i, j: (0, 0)),
                  pl.BlockSpec((1, d), lambda b, i, j: (0, 0)),
                  pl.BlockSpec((d, tn), lambda b, i, j: (0, j))],
        out_specs=pl.BlockSpec((1, tm, tn), lambda b, i, j: (b, i, j)),
        out_shape=jax.ShapeDtypeStruct((nb, nt, P_W), F32),
        scratch_shapes=[pltpu.VMEM((tm, d), BF16)],
        compiler_params=_cparams("parallel", "parallel", "arbitrary"),
        name="norm_proj",
    )(xa, mod, norm_g.reshape(1, d), w_pad)


def _aprep_kernel(qk_ref, v_ref, cos_ref, sin_ref, qt_ref, k_ref, vt_ref):
    cosv = cos_ref[...]
    sinv = sin_ref[...]
    lane = lax.broadcasted_iota(jnp.int32, (1, 128), 1)
    even = (lane % 2) == 0
    first = lane < DA_DH

    def rope(x):
        swapped = jnp.where(even, pltpu.roll(x, 127, 1), pltpu.roll(x, 1, 1))
        return x * cosv + swapped * sinv

    for h in range(DA_HEADS):
        q = rope(qk_ref[0, :, h * 128:(h + 1) * 128]) * (DA_DH ** -0.5)
        k = rope(qk_ref[0, :, BR_W + h * 128:BR_W + (h + 1) * 128])
        k_ref[0, :, h * 128:(h + 1) * 128] = k.astype(BF16)
        qt_ref[0, 2 * h] = jnp.where(first, q, 0.0).T.astype(BF16)
        qt_ref[0, 2 * h + 1] = jnp.where(first, 0.0, q).T.astype(BF16)
        vt_ref[0, 0, h * 128:(h + 1) * 128, :] = v_ref[0, :, h * 128:(h + 1) * 128].T.astype(BF16)


def _attn_prep(p, cos_t, sin_t):
    nb, nt, _ = p.shape
    tm = ROW_TILE
    nti = nt // tm
    return pl.pallas_call(
        _aprep_kernel,
        grid=(nb, nti),
        in_specs=[pl.BlockSpec((1, tm, 2 * BR_W), lambda b, i: (b, i, 0)),
                  pl.BlockSpec((1, tm, BR_W), lambda b, i: (b, i, U_AV)),
                  pl.BlockSpec((tm, 128), lambda b, i: (i, 0)),
                  pl.BlockSpec((tm, 128), lambda b, i: (i, 0))],
        out_specs=[pl.BlockSpec((1, 2 * DA_HEADS, 128, tm), lambda b, i: (b, 0, 0, i)),
                   pl.BlockSpec((1, tm, BR_W), lambda b, i: (b, i, 0)),
                   pl.BlockSpec((1, 1, BR_W, tm), lambda b, i: (b, i, 0, 0))],
        out_shape=[jax.ShapeDtypeStruct((nb, 2 * DA_HEADS, 128, nt), BF16),
                   jax.ShapeDtypeStruct((nb, nt, BR_W), BF16),
                   jax.ShapeDtypeStruct((nb, nti, BR_W, tm), BF16)],
        compiler_params=_cparams("parallel", "parallel"),
        name="attn_prep",
    )(p, p, cos_t, sin_t)


def _lambda(lq_ref, lk_ref, lam_init):
    e = jnp.exp(jnp.sum(lq_ref[...] * lk_ref[...], axis=-1, keepdims=True))
    return e[0:1] - e[1:2] + lam_init


def _attn_kernel(lq_ref, lk_ref, g_ref, q1_ref, q2_ref, k_ref, vt_ref, z_ref, o_ref, acc1, acc2,
                 *, nkv, tk, lam_init):
    q1 = q1_ref[0, 0]
    q2 = q2_ref[0, 0]
    tq = q1.shape[1]
    acc1[...] = jnp.zeros_like(acc1)
    acc2[...] = jnp.zeros_like(acc2)

    def body(j, carry):
        m1, l1, m2, l2 = carry
        kk = k_ref[0, pl.ds(pl.multiple_of(j * tk, tk), tk), :]
        vv = vt_ref[0, j]

        def update(q, m, l, acc):
            s = jnp.dot(kk, q, preferred_element_type=F32)
            mn = jnp.maximum(m, jnp.max(s, axis=0, keepdims=True))
            a = jnp.exp(m - mn)
            p = jnp.exp(s - mn)
            l = a * l + jnp.sum(p, axis=0, keepdims=True)
            acc[...] = acc[...] * a + jnp.dot(vv, p.astype(BF16), preferred_element_type=F32)
            return mn, l

        m1, l1 = update(q1, m1, l1, acc1)
        m2, l2 = update(q2, m2, l2, acc2)
        return m1, l1, m2, l2

    neg = jnp.full((1, tq), NEG, F32)
    zero = jnp.zeros((1, tq), F32)
    _, l1, _, l2 = lax.fori_loop(0, nkv, body, (neg, zero, neg, zero))

    lam = _lambda(lq_ref, lk_ref, lam_init)
    o = acc1[...] / l1 - lam * (acc2[...] / l2)
    ms = jnp.mean(o * o, axis=0, keepdims=True)
    y = (o * lax.rsqrt(ms + EPS)).T * g_ref[...] * (1.0 - lam_init)
    o_ref[0] = (y * _silu(z_ref[0])).astype(o_ref.dtype)


def _diff_attention(p, qt, kr, vt, lam_q, lam_k, da_g, lam_init, n_lat, tq):
    nb, nt, _ = p.shape
    nkv, tk = vt.shape[1], vt.shape[3]
    return pl.pallas_call(
        functools.partial(_attn_kernel, nkv=nkv, tk=tk, lam_init=lam_init),
        grid=(nb, DA_HEADS, n_lat // tq),
        in_specs=[pl.BlockSpec((2, DA_DH), lambda b, h, i: (0, 0)),
                  pl.BlockSpec((2, DA_DH), lambda b, h, i: (0, 0)),
                  pl.BlockSpec((1, DA_DV), lambda b, h, i: (0, 0)),
                  pl.BlockSpec((1, 1, 128, tq), lambda b, h, i: (b, 2 * h, 0, i)),
                  pl.BlockSpec((1, 1, 128, tq), lambda b, h, i: (b, 2 * h + 1, 0, i)),
                  pl.BlockSpec((1, nt, 128), lambda b, h, i: (b, 0, h)),
                  pl.BlockSpec((1, nkv, 128, tk), lambda b, h, i: (b, 0, h, 0)),
                  pl.BlockSpec((1, tq, 128), lambda b, h, i: (b, i, U_AZ * 4 + h))],
        out_specs=pl.BlockSpec((1, tq, 128), lambda b, h, i: (b, i, h)),
        out_shape=jax.ShapeDtypeStruct((nb, nt, BR_W), BF16),
        scratch_shapes=[pltpu.VMEM((128, tq), F32), pltpu.VMEM((128, tq), F32)],
        compiler_params=_cparams("parallel", "parallel", "parallel"),
        name="diff_attention",
    )(lam_q, lam_k, da_g.reshape(1, DA_DV), qt, qt, kr, vt, p)


def _log_sigmoid(x):
    return jnp.minimum(x, 0.0) - jnp.log1p(jnp.exp(-jnp.abs(x)))


def _bprep_kernel(x_ref, prev_ref, next_ref, g_ref, w_ref, cb_ref, gb_ref, qk_ref, gl_ref, *, n_lat, n_tok, tm):
    i = pl.program_id(1)
    x = x_ref[0]
    row = lax.broadcasted_iota(jnp.int32, (tm, 1), 0)
    pos = i * tm + row
    xp = jnp.where(row == 0, prev_ref[0, 7:8, :], pltpu.roll(x, 1, 0))
    xp = jnp.where((pos == 0) | (pos == n_lat), 0.0, xp)
    xn = jnp.where(row == tm - 1, next_ref[0, 0:1, :], pltpu.roll(x, tm - 1, 0))
    xn = jnp.where((pos == n_lat - 1) | (pos == n_tok - 1), 0.0, xn)
    y = xp * w_ref[0:1, :] + x * w_ref[1:2, :] + xn * w_ref[2:3, :] + cb_ref[...]
    y = _silu(y)
    lane = lax.broadcasted_iota(jnp.int32, (1, 2 * BR_W), 1)
    qk_ref[0] = jnp.where(lane >= BR_W, y * (ML_DH ** -0.5), y).astype(BF16)
    g = g_ref[0] + gb_ref[...]
    gl_lane = lax.broadcasted_iota(jnp.int32, (1, 128), 1)
    is_f = (gl_lane % (2 * ML_HEADS)) >= ML_HEADS
    gl_ref[0] = jnp.where(is_f, _log_sigmoid(g), g)


def _mlstm_prep(p, conv_w, conv_b, gate_b, n_lat):
    nb, nt, _ = p.shape
    tm = ROW_TILE
    hb = tm // 8
    last = nt // 8 - 1
    gb = jnp.zeros((1, 128), F32).at[0, :N_GATE].set(gate_b.reshape(-1))
    return pl.pallas_call(
        functools.partial(_bprep_kernel, n_lat=n_lat, n_tok=nt, tm=tm),
        grid=(nb, nt // tm),
        in_specs=[pl.BlockSpec((1, tm, 2 * BR_W), lambda b, i: (b, i, U_BQ // 2)),
                  pl.BlockSpec((1, 8, 2 * BR_W), lambda b, i: (b, jnp.maximum(i * hb - 1, 0), U_BQ // 2)),
                  pl.BlockSpec((1, 8, 2 * BR_W), lambda b, i: (b, jnp.minimum((i + 1) * hb, last), U_BQ // 2)),
                  pl.BlockSpec((1, tm, 128), lambda b, i: (b, i, U_BG * 4)),
                  pl.BlockSpec((3, 2 * BR_W), lambda b, i: (0, 0)),
                  pl.BlockSpec((1, 2 * BR_W), lambda b, i: (0, 0)),
                  pl.BlockSpec((1, 128), lambda b, i: (0, 0))],
        out_specs=[pl.BlockSpec((1, tm, 2 * BR_W), lambda b, i: (b, i, 0)),
                   pl.BlockSpec((1, tm, 128), lambda b, i: (b, i, 0))],
        out_shape=[jax.ShapeDtypeStruct((nb, nt, 2 * BR_W), BF16),
                   jax.ShapeDtypeStruct((nb, nt, 128), F32)],
        compiler_params=_cparams("parallel", "parallel"),
        name="mlstm_prep",
    )(p, p, p, p, conv_w, conv_b.reshape(1, -1), gb)


def _split_dot(a, b_f32, a_is_tri):
    hi = b_f32.astype(BF16)
    lo = (b_f32 - hi.astype(F32)).astype(BF16)
    if a_is_tri:
        return jnp.dot(a, hi, preferred_element_type=F32) + jnp.dot(a, lo, preferred_element_type=F32)
    return jnp.dot(hi, a, preferred_element_type=F32) + jnp.dot(lo, a, preferred_element_type=F32)


def _scan_kernel(qkf_ref, qkb_ref, vf_ref, vb_ref, glf_ref, glb_ref, gtf_ref, gtb_ref,
                 hf_ref, hb_ref, c_scr, n_scr, m_scr):
    t = pl.program_id(1)
    L = ML_CHUNK

    @pl.when(t == 0)
    def _():
        c_scr[...] = jnp.zeros_like(c_scr)
        n_scr[...] = jnp.zeros_like(n_scr)
        m_scr[...] = jnp.zeros_like(m_scr)

    ti = lax.broadcasted_iota(jnp.int32, (L, L), 0)
    si = lax.broadcasted_iota(jnp.int32, (L, L), 1)
    for d in range(2):
        qk_ref, v_ref, gl_ref, gt_ref, h_ref = ((qkf_ref, vf_ref, glf_ref, gtf_ref, hf_ref) if d == 0 else
                                                (qkb_ref, vb_ref, glb_ref, gtb_ref, hb_ref))
        seen = (si <= ti) if d == 0 else (si >= ti)
        tri = seen.astype(BF16)
        tri_t = ((ti <= si) if d == 0 else (ti >= si)).astype(BF16)
        gl = gl_ref[0]
        gt = gt_ref[0, 0]
        bc_all = _split_dot(tri, gl, True)
        br_all = _split_dot(tri_t, gt, False)
        last = L - 1 if d == 0 else 0
        for h in range(ML_HEADS):
            ci = d * 2 * ML_HEADS + h
            cf = ci + ML_HEADS
            idx = d * ML_HEADS + h
            q = qk_ref[0, :, h * ML_DH:(h + 1) * ML_DH]
            k = qk_ref[0, :, BR_W + h * ML_DH:BR_W + (h + 1) * ML_DH]
            v = v_ref[0, :, h * ML_DH:(h + 1) * ML_DH].astype(BF16)
            li_c = gl[:, ci:ci + 1]
            b_c = bc_all[:, cf:cf + 1]
            li_r = gt[ci:ci + 1, :]
            b_r = br_all[cf:cf + 1, :]
            b_tot = b_c[last:last + 1, :]
            cmat = c_scr[idx]
            nvec = n_scr[idx]
            m = m_scr[idx][:, 0:1]

            g_c = b_tot - b_c + li_c
            m_new = jnp.maximum(b_tot + m, jnp.max(g_c, axis=0, keepdims=True))
            w_c = jnp.exp(g_c - m_new)
            decay = jnp.exp(b_tot + m - m_new)
            wk = k.astype(F32) * w_c
            c_new = decay * cmat + jnp.dot(wk.T.astype(BF16), v, preferred_element_type=F32)
            n_new = decay * nvec + jnp.sum(wk, axis=0, keepdims=True)

            dmat = jnp.where(seen, b_c - b_r + li_r, NEG)
            m_inter = b_c + m
            m_t = jnp.maximum(m_inter, jnp.max(dmat, axis=-1, keepdims=True))
            a_inter = jnp.exp(m_inter - m_t)
            s = lax.dot_general(q, k, (((1,), (1,)), ((), ())), preferred_element_type=F32)
            s = s * jnp.exp(dmat - m_t)
            num = a_inter * jnp.dot(q, cmat.astype(BF16), preferred_element_type=F32) \
                + jnp.dot(s.astype(BF16), v, preferred_element_type=F32)
            den = a_inter * jnp.sum(q.astype(F32) * nvec, axis=-1, keepdims=True) \
                + jnp.sum(s, axis=-1, keepdims=True)
            hout = num / jnp.maximum(jnp.abs(den), jnp.exp(-m_t))
            h_ref[0, :, h * ML_DH:(h + 1) * ML_DH] = hout

            c_scr[idx] = c_new
            n_scr[idx] = n_new
            m_scr[idx] = jnp.broadcast_to(m_new, (1, 128))


def _mlstm_scan(p, qkc, gl, glt, n_lat):
    nb, nt, _ = p.shape
    L = ML_CHUNK
    nch = nt // L
    nlc = n_lat // L
    fwd = lambda t: (t + nlc) % nch
    bwd = lambda t: nch - 1 - t
    return pl.pallas_call(
        _scan_kernel,
        grid=(nb, nch),
        in_specs=[pl.BlockSpec((1, L, 2 * BR_W), lambda b, t: (b, fwd(t), 0)),
                  pl.BlockSpec((1, L, 2 * BR_W), lambda b, t: (b, bwd(t), 0)),
                  pl.BlockSpec((1, L, BR_W), lambda b, t: (b, fwd(t), U_BV)),
                  pl.BlockSpec((1, L, BR_W), lambda b, t: (b, bwd(t), U_BV)),
                  pl.BlockSpec((1, L, 128), lambda b, t: (b, fwd(t), 0)),
                  pl.BlockSpec((1, L, 128), lambda b, t: (b, bwd(t), 0)),
                  pl.BlockSpec((1, 1, N_GATE, L), lambda b, t: (b, fwd(t), 0, 0)),
                  pl.BlockSpec((1, 1, N_GATE, L), lambda b, t: (b, bwd(t), 0, 0))],
        out_specs=[pl.BlockSpec((1, L, BR_W), lambda b, t: (b, fwd(t), 0)),
                   pl.BlockSpec((1, L, BR_W), lambda b, t: (b, bwd(t), 0))],
        out_shape=[jax.ShapeDtypeStruct((nb, nt, BR_W), F32),
                   jax.ShapeDtypeStruct((nb, nt, BR_W), F32)],
        scratch_shapes=[pltpu.VMEM((2 * ML_HEADS, ML_DH, ML_DH), F32),
                        pltpu.VMEM((2 * ML_HEADS, 1, ML_DH), F32),
                        pltpu.VMEM((2 * ML_HEADS, 1, 128), F32)],
        compiler_params=_cparams("parallel", "arbitrary"),
        name="mlstm_scan",
    )(qkc, qkc, p, p, gl, gl, glt, glt)


NA_RB = 8
NA_BLK = NA_RB * GRID_W


def _na_bias_table(rpb):
    c = np.arange(GRID_W)
    kcol = np.arange(GRID_W)
    cs = np.clip(c - NA_KW // 2, 0, GRID_W - NA_KW)
    ok = (kcol[None, :] >= cs[:, None]) & (kcol[None, :] < cs[:, None] + NA_KW)
    cidx = np.clip(kcol[None, :] - c[:, None] + NA_KW - 1, 0, 2 * NA_KW - 2)
    ridx = np.arange(NA_KH)[:, None] + np.arange(NA_KH)[None, :]
    tab = rpb[:, ridx[:, :, None, None], cidx[None, None, :, :]]
    tab = jnp.where(jnp.asarray(ok)[None, None, None], tab.astype(F32), NEG)
    tab = jnp.transpose(tab, (1, 0, 3, 2, 4))
    return tab.reshape(NA_KH, NA_HEADS, GRID_W, NA_KH * GRID_W)


def _na_kernel(q_ref, kp_ref, kc_ref, kn_ref, vp_ref, vc_ref, vn_ref, kx_ref, vx_ref, z_ref, bias_ref,
               o_ref, ks, vs, kxs, vxs, *, rows):
    i = pl.program_id(1)
    nb = NA_BLK
    ks[0:nb] = kp_ref[0].astype(BF16)
    ks[nb:2 * nb] = kc_ref[0].astype(BF16)
    ks[2 * nb:3 * nb] = kn_ref[0].astype(BF16)
    vs[0:nb] = vp_ref[0].astype(BF16)
    vs[nb:2 * nb] = vc_ref[0].astype(BF16)
    vs[2 * nb:3 * nb] = vn_ref[0].astype(BF16)
    kxs[...] = kx_ref[0].astype(BF16)
    vxs[...] = vx_ref[0].astype(BF16)
    first = lax.broadcasted_iota(jnp.int32, (1, 128), 1) < NA_DH
    nloc = NA_KH * GRID_W
    nt_dims = (((1,), (1,)), ((), ()))

    def row_body(rr, carry):
        r = i * NA_RB + rr
        rs = jnp.clip(r - NA_KH // 2, 0, rows - NA_KH)
        off = pl.multiple_of((rs - (i - 1) * NA_RB) * GRID_W, GRID_W)
        j0 = rs - r + NA_KH - 1
        qoff = pl.multiple_of(rr * GRID_W, GRID_W)
        qr = q_ref[0, pl.ds(qoff, GRID_W), :] * (NA_DH ** -0.5)
        outs = []
        for pr in range(NA_HEADS // 2):
            cols = slice(pr * 128, (pr + 1) * 128)
            qp = qr[:, cols]
            kl = ks[pl.ds(off, nloc), cols]
            vl = vs[pl.ds(off, nloc), cols]
            kx = kxs[:, cols]
            vx = vxs[:, cols]
            pair = []
            for hh in range(2):
                qm = jnp.where(first if hh == 0 else jnp.logical_not(first), qp, 0.0).astype(BF16)
                s_loc = lax.dot_general(qm, kl, nt_dims, preferred_element_type=F32) + bias_ref[j0, 2 * pr + hh]
                s_ctx = lax.dot_general(qm, kx, nt_dims, preferred_element_type=F32)
                m = jnp.maximum(jnp.max(s_loc, axis=-1, keepdims=True), jnp.max(s_ctx, axis=-1, keepdims=True))
                p_loc = jnp.exp(s_loc - m)
                p_ctx = jnp.exp(s_ctx - m)
                l = jnp.sum(p_loc, axis=-1, keepdims=True) + jnp.sum(p_ctx, axis=-1, keepdims=True)
                o = jnp.dot(p_loc.astype(BF16), vl, preferred_element_type=F32) \
                    + jnp.dot(p_ctx.astype(BF16), vx, preferred_element_type=F32)
                pair.append(o / l)
            outs.append(jnp.where(first, pair[0], pair[1]))
        o_all = jnp.concatenate(outs, axis=-1)
        z = z_ref[0, pl.ds(qoff, GRID_W), :]
        o_ref[0, pl.ds(qoff, GRID_W), :] = (o_all * _silu(z)).astype(o_ref.dtype)
        return carry

    lax.fori_loop(0, NA_RB, row_body, 0)


def _neighbourhood_attention(p, bias_tab, n_lat, n_ctx):
    nb, nt, _ = p.shape
    rows = n_lat // GRID_W
    nblk = rows // NA_RB
    blk = NA_BLK
    cblk = n_lat // n_ctx
    spec = lambda unit, f: pl.BlockSpec((1, blk, BR_W), lambda b, i: (b, f(i), unit))
    prev = lambda i: jnp.maximum(i - 1, 0)
    cur = lambda i: i
    nxt = lambda i: jnp.minimum(i + 1, nblk - 1)
    return pl.pallas_call(
        functools.partial(_na_kernel, rows=rows),
        grid=(nb, nblk),
        in_specs=[spec(U_CQ, cur),
                  spec(U_CK, prev), spec(U_CK, cur), spec(U_CK, nxt),
                  spec(U_CV, prev), spec(U_CV, cur), spec(U_CV, nxt),
                  pl.BlockSpec((1, n_ctx, BR_W), lambda b, i: (b, cblk, U_CK)),
                  pl.BlockSpec((1, n_ctx, BR_W), lambda b, i: (b, cblk, U_CV)),
                  spec(U_CZ, cur),
                  pl.BlockSpec((NA_KH, NA_HEADS, GRID_W, NA_KH * GRID_W), lambda b, i: (0, 0, 0, 0))],
        out_specs=pl.BlockSpec((1, blk, BR_W), lambda b, i: (b, i, 0)),
        out_shape=jax.ShapeDtypeStruct((nb, nt, BR_W), BF16),
        scratch_shapes=[pltpu.VMEM((3 * blk, BR_W), BF16), pltpu.VMEM((3 * blk, BR_W), BF16),
                        pltpu.VMEM((n_ctx, BR_W), BF16), pltpu.VMEM((n_ctx, BR_W), BF16)],
        compiler_params=_cparams("parallel", "parallel"),
        name="neighbourhood_attention",
    )(p, p, p, p, p, p, p, p, p, p, bias_tab)


def _ctx_kernel(lq_ref, lk_ref, g_ref, aqk_ref, avz_ref, cqk_ref, cvz_ref, ya_in, yc_in, ya_ref, yc_ref, *, lam_init):
    del ya_in, yc_in
    lam = _lambda(lq_ref, lk_ref, lam_init)
    first = lax.broadcasted_iota(jnp.int32, (1, 128), 1) < 64
    second = jnp.logical_not(first)
    nt_dims = (((1,), (1,)), ((), ()))

    def softmax(s):
        e = jnp.exp(s - jnp.max(s, axis=-1, keepdims=True))
        return e / jnp.sum(e, axis=-1, keepdims=True)

    outs = []
    for h in range(DA_HEADS):
        cols = slice(h * 128, (h + 1) * 128)
        q = aqk_ref[0, :, cols] * (DA_DH ** -0.5)
        k = aqk_ref[0, :, BR_W + h * 128:BR_W + (h + 1) * 128].astype(BF16)
        v = avz_ref[0, :, cols].astype(BF16)
        z = avz_ref[0, :, BR_W + h * 128:BR_W + (h + 1) * 128]
        s1 = lax.dot_general(jnp.where(first, q, 0.0).astype(BF16), k, nt_dims, preferred_element_type=F32)
        s2 = lax.dot_general(jnp.where(second, q, 0.0).astype(BF16), k, nt_dims, preferred_element_type=F32)
        a = softmax(s1) - lam * softmax(s2)
        o = jnp.dot(a.astype(BF16), v, preferred_element_type=F32)
        ms = jnp.mean(o * o, axis=-1, keepdims=True)
        y = o * lax.rsqrt(ms + EPS) * g_ref[...] * (1.0 - lam_init)
        outs.append(y * _silu(z))
    ya_ref[0] = jnp.concatenate(outs, axis=-1).astype(ya_ref.dtype)

    outs = []
    for pr in range(NA_HEADS // 2):
        cols = slice(pr * 128, (pr + 1) * 128)
        q = cqk_ref[0, :, cols] * (NA_DH ** -0.5)
        k = cqk_ref[0, :, BR_W + pr * 128:BR_W + (pr + 1) * 128].astype(BF16)
        v = cvz_ref[0, :, cols].astype(BF16)
        pair = []
        for msk in (first, second):
            s = lax.dot_general(jnp.where(msk, q, 0.0).astype(BF16), k, nt_dims, preferred_element_type=F32)
            pair.append(jnp.dot(softmax(s).astype(BF16), v, preferred_element_type=F32))
        outs.append(jnp.where(first, pair[0], pair[1]))
    z = cvz_ref[0, :, BR_W:]
    yc_ref[0] = (jnp.concatenate(outs, axis=-1) * _silu(z)).astype(yc_ref.dtype)


def _ctx_attention(p, ya, yc, lam_q, lam_k, da_g, lam_init, n_lat, n_ctx):
    nb, nt, _ = p.shape
    cblk = n_lat // n_ctx
    pspec = lambda unit: pl.BlockSpec((1, n_ctx, 2 * BR_W), lambda b: (b, cblk, unit // 2))
    yspec = pl.BlockSpec((1, n_ctx, BR_W), lambda b: (b, cblk, 0))
    small = lambda shape: pl.BlockSpec(shape, lambda b: (0, 0))
    return pl.pallas_call(
        functools.partial(_ctx_kernel, lam_init=lam_init),
        grid=(nb,),
        in_specs=[small((2, DA_DH)), small((2, DA_DH)), small((1, DA_DV)),
                  pspec(U_AQ), pspec(U_AV), pspec(U_CQ), pspec(U_CV),
                  pl.BlockSpec(memory_space=pl.ANY), pl.BlockSpec(memory_space=pl.ANY)],
        out_specs=[yspec, yspec],
        out_shape=[jax.ShapeDtypeStruct(ya.shape, ya.dtype), jax.ShapeDtypeStruct(yc.shape, yc.dtype)],
        input_output_aliases={7: 0, 8: 1},
        compiler_params=_cparams("parallel"),
        name="ctx_attention",
    )(lam_q, lam_k, da_g.reshape(1, DA_DV), p, p, p, p, ya, yc)


def _merge_kernel(x_ref, ya_ref, yc_ref, hf_ref, hb_ref, bo_ref, bz_ref, gm0_ref, gm1_ref, gm2_ref,
                  mod_ref, mlg_ref, fg_ref, wbr_ref, wout_ref, o_ref, *, n_lat, n_batch, tm, final_norm):
    b = pl.program_id(0)
    i = pl.program_id(1)
    hsum = hf_ref[0] + hb_ref[0]
    parts = []
    for h in range(ML_HEADS):
        hh = hsum[:, h * ML_DH:(h + 1) * ML_DH]
        ms = jnp.mean(hh * hh, axis=-1, keepdims=True)
        parts.append(hh * lax.rsqrt(ms + EPS) * mlg_ref[...])
    hn = jnp.concatenate(parts, axis=-1)
    yb = (jax.nn.sigmoid(bo_ref[0]) * hn * _silu(bz_ref[0])).astype(BF16)

    merged = jax.nn.sigmoid(gm0_ref[0]) * jnp.dot(ya_ref[0], wbr_ref[0], preferred_element_type=F32)
    merged += jax.nn.sigmoid(gm1_ref[0]) * jnp.dot(yb, wbr_ref[1], preferred_element_type=F32)
    merged += jax.nn.sigmoid(gm2_ref[0]) * jnp.dot(yc_ref[0], wbr_ref[2], preferred_element_type=F32)
    upd = jnp.dot(merged.astype(BF16), wout_ref[...], preferred_element_type=F32)

    row = i * tm + lax.broadcasted_iota(jnp.int32, (tm, 1), 0)
    gate = jnp.where(row >= n_lat, mod_ref[n_batch:n_batch + 1, 2 * D_MODEL:], mod_ref[pl.ds(b, 1), 2 * D_MODEL:])
    xo = x_ref[0] + gate * upd
    if final_norm:
        ms = jnp.mean(xo * xo, axis=-1, keepdims=True)
        xo = xo * lax.rsqrt(ms + EPS) * fg_ref[...]
    o_ref[0] = xo


def _merge(xa, p, ya, yc, hf, hb, mod, ml_g, final_g, wbr, wout, n_lat, n_rows, tm, final_norm):
    nb, nt, d = xa.shape
    row = lambda width, unit: pl.BlockSpec((1, tm, width), lambda b, i: (b, i, unit))
    const = lambda shape: pl.BlockSpec(shape, lambda b, i: (0,) * len(shape))
    return pl.pallas_call(
        functools.partial(_merge_kernel, n_lat=n_lat, n_batch=nb, tm=tm, final_norm=final_norm),
        grid=(nb, n_rows // tm),
        in_specs=[row(d, 0), row(BR_W, 0), row(BR_W, 0), row(BR_W, 0), row(BR_W, 0),
                  row(BR_W, U_BO), row(BR_W, U_BZ),
                  row(d, U_GM // 2), row(d, U_GM // 2 + 1), row(d, U_GM // 2 + 2),
                  const((8, 3 * d)), const((1, ML_DH)), const((1, d)),
                  const((3, BR_W, d)), const((d, d))],
        out_specs=row(d, 0),
        out_shape=jax.ShapeDtypeStruct((nb, n_rows, d), F32),
        compiler_params=_cparams("parallel", "parallel"),
        name="merge",
    )(xa, ya, yc, hf, hb, p, p, p, p, p, mod, ml_g.reshape(1, ML_DH), final_g.reshape(1, d), wbr, wout)


def _rope_tables(n_lat, n_tok):
    t = jnp.arange(n_lat, dtype=jnp.int32)
    row = (t // GRID_W).astype(F32)
    col = (t % GRID_W).astype(F32)
    inv = ROPE_BASE ** (-jnp.arange(0, DA_DH // 2, 2, dtype=F32) / (DA_DH // 2))
    ang = jnp.concatenate([row[:, None] * inv, col[:, None] * inv], axis=-1)
    cos = jnp.repeat(jnp.cos(ang), 2, axis=-1)
    sin = jnp.repeat(jnp.sin(ang), 2, axis=-1) * jnp.tile(jnp.asarray([-1.0, 1.0], F32), DA_DH // 2)
    cos = jnp.tile(cos, (1, 2))
    sin = jnp.tile(sin, (1, 2))
    cos = jnp.concatenate([cos, jnp.ones((n_tok - n_lat, 128), F32)], axis=0)
    sin = jnp.concatenate([sin, jnp.zeros((n_tok - n_lat, 128), F32)], axis=0)
    return cos, sin


def _pad_w_in(w):
    d = w.shape[0]
    split = 9 * BR_W + N_GATE
    return jnp.concatenate([w[:, :split], jnp.zeros((d, GATE_PAD - N_GATE), w.dtype), w[:, split:]], axis=1).astype(BF16)


def kernel(x, c, ctx, c_ctx, w_mod, b_mod, norm_g, w_in, da_lam_q, da_lam_k, da_norm_g, ml_conv_w, ml_conv_b,
           ml_gate_b, ml_norm_g, na_rpb, w_br, w_out, final_g):
    nb, n_lat, d = x.shape
    n_ctx = ctx.shape[1]
    nt = n_lat + n_ctx
    depth = w_mod.shape[0]
    assert d == D_MODEL and nt % ROW_TILE == 0 and n_lat % n_ctx == 0 and n_lat % NA_BLK == 0 and nb + 1 <= 8
    tq = 512

    cs = jnp.zeros((8, d), F32).at[:nb].set(c).at[nb].set(c_ctx)
    mods = _modulation(cs, w_mod, b_mod)
    cos_t, sin_t = _rope_tables(n_lat, nt)
    xa = jnp.concatenate([x, ctx], axis=1)

    out = None
    for l in range(depth):
        last = l == depth - 1
        lam_init = 0.8 - 0.6 * math.exp(-0.3 * l)
        p = _projection(xa, mods[l], norm_g[l], _pad_w_in(w_in[l]), n_lat)

        qt, kr, vt = _attn_prep(p, cos_t, sin_t)
        ya = _diff_attention(p, qt, kr, vt, da_lam_q[l], da_lam_k[l], da_norm_g[l], lam_init, n_lat, tq)

        qkc, gl = _mlstm_prep(p, ml_conv_w[l], ml_conv_b[l], ml_gate_b[l], n_lat)
        glt = jnp.transpose(gl[:, :, :N_GATE].reshape(nb, nt // ML_CHUNK, ML_CHUNK, N_GATE), (0, 1, 3, 2))
        hf, hb = _mlstm_scan(p, qkc, gl, glt, n_lat)

        yc = _neighbourhood_attention(p, _na_bias_table(na_rpb[l]), n_lat, n_ctx)

        wbr = w_br[l].astype(BF16)
        wout = w_out[l].astype(BF16)
        if not last:
            ya, yc = _ctx_attention(p, ya, yc, da_lam_q[l], da_lam_k[l], da_norm_g[l], lam_init, n_lat, n_ctx)
            xa = _merge(xa, p, ya, yc, hf, hb, mods[l], ml_norm_g[l], final_g, wbr, wout, n_lat, nt, 384, False)
        else:
            out = _merge(xa, p, ya, yc, hf, hb, mods[l], ml_norm_g[l], final_g, wbr, wout, n_lat, n_lat, 512, True)
    return out
```

```python
import functools
import math

import numpy as np
import jax
import jax.numpy as jnp
from jax import lax
from jax.experimental import pallas as pl
from jax.experimental.pallas import tpu as pltpu

F32 = jnp.float32
BF16 = jnp.bfloat16

D_MODEL = 1024
BR_W = 512
GRID_W = 64
EPS = 1e-6
ROPE_BASE = 10000.0
DA_HEADS, DA_DH, DA_DV = 4, 64, 128
ML_HEADS, ML_DH, ML_CHUNK = 4, 128, 64
NA_HEADS, NA_DH, NA_KH, NA_KW = 8, 64, 8, 16
N_GATE = 4 * ML_HEADS
GATE_PAD = 512
P_W = 9 * BR_W + GATE_PAD + 4 * BR_W + 3 * D_MODEL
U_AQ, U_AK, U_AV, U_AZ, U_BQ, U_BK, U_BV, U_BO, U_BZ, U_BG, U_CQ, U_CK, U_CV, U_CZ, U_GM = range(15)
NEG = -1e30
LOG2E = math.log2(math.e)
VMEM_LIMIT = 56 * 1024 * 1024
ROW_TILE = 768


def _cparams(*sem):
    return pltpu.CompilerParams(dimension_semantics=sem, vmem_limit_bytes=VMEM_LIMIT)


def _silu(x):
    return x * jax.nn.sigmoid(x)


def _mod_kernel(c_ref, w_ref, b_ref, o_ref):
    s = _silu(c_ref[...])
    o_ref[0] = jnp.dot(s.astype(BF16), w_ref[0].astype(BF16), preferred_element_type=F32) + b_ref[0]


def _modulation(cs, w_mod, b_mod):
    depth, d, d3 = w_mod.shape
    tn = 1024
    return pl.pallas_call(
        _mod_kernel,
        grid=(depth, d3 // tn),
        in_specs=[pl.BlockSpec((8, d), lambda l, j: (0, 0)),
                  pl.BlockSpec((1, d, tn), lambda l, j: (l, 0, j)),
                  pl.BlockSpec((1, 1, tn), lambda l, j: (l, 0, j))],
        out_specs=pl.BlockSpec((1, 8, tn), lambda l, j: (l, 0, j)),
        out_shape=jax.ShapeDtypeStruct((depth, 8, d3), F32),
        compiler_params=_cparams("parallel", "parallel"),
        name="modulation",
    )(cs, w_mod, b_mod.reshape(depth, 1, d3))


PROJ_TN = 1024
GATE_TILE = (U_BG * BR_W) // PROJ_TN
GATE_OFF = (U_BG * BR_W) % PROJ_TN


def _proj_kernel(x_ref, mod_ref, g_ref, w_ref, o_ref, gate_ref, h_scr, *, n_lat, n_batch, tm):
    b = pl.program_id(0)
    i = pl.program_id(1)
    j = pl.program_id(2)

    @pl.when(j == 0)
    def _():
        x = x_ref[0]
        ms = jnp.mean(x * x, axis=-1, keepdims=True)
        y = x * lax.rsqrt(ms + EPS) * g_ref[...]
        row = i * tm + lax.broadcasted_iota(jnp.int32, (tm, 1), 0)
        is_ctx = row >= n_lat
        mb = mod_ref[pl.ds(b, 1), :]
        mc = mod_ref[n_batch:n_batch + 1, :]
        shift = jnp.where(is_ctx, mc[:, :D_MODEL], mb[:, :D_MODEL])
        scale = jnp.where(is_ctx, mc[:, D_MODEL:2 * D_MODEL], mb[:, D_MODEL:2 * D_MODEL])
        h_scr[...] = (y * (1.0 + scale) + shift).astype(BF16)

    acc = jnp.dot(h_scr[...], w_ref[...], preferred_element_type=F32)
    o_ref[0] = acc.astype(BF16)

    @pl.when(j == GATE_TILE)
    def _():
        gate_ref[0] = acc[:, GATE_OFF:GATE_OFF + 128]


def _projection(xa, mod, norm_g, w_pad, n_lat):
    nb, nt, d = xa.shape
    tm, tn = ROW_TILE, PROJ_TN
    return pl.pallas_call(
        functools.partial(_proj_kernel, n_lat=n_lat, n_batch=nb, tm=tm),
        grid=(nb, nt // tm, P_W // tn),
        in_specs=[pl.BlockSpec((1, tm, d), lambda b, i, j: (b, i, 0)),
                  pl.BlockSpec((8, 3 * d), lambda b, i, j: (0, 0)),
                  pl.BlockSpec((1, d), lambda b, i, j: (0, 0)),
                  pl.BlockSpec((d, tn), lambda b, i, j: (0, j))],
        out_specs=[pl.BlockSpec((1, tm, tn), lambda b, i, j: (b, i, j)),
                   pl.BlockSpec((1, tm, 128), lambda b, i, j: (b, i, 0))],
        out_shape=[jax.ShapeDtypeStruct((nb, nt, P_W), BF16),
                   jax.ShapeDtypeStruct((nb, nt, 128), F32)],
        scratch_shapes=[pltpu.VMEM((tm, d), BF16)],
        compiler_params=_cparams("parallel", "parallel", "arbitrary"),
        name="norm_proj",
    )(xa, mod, norm_g.reshape(1, d), w_pad)


def _aprep_kernel(qk_ref, v_ref, cos_ref, sin_ref, qt_ref, k_ref, vt_ref):
    cosv = cos_ref[...]
    sinv = sin_ref[...]
    lane = lax.broadcasted_iota(jnp.int32, (1, 128), 1)
    even = (lane % 2) == 0
    first = lane < DA_DH

    def rope(x):
        swapped = jnp.where(even, pltpu.roll(x, 127, 1), pltpu.roll(x, 1, 1))
        return x * cosv + swapped * sinv

    for h in range(DA_HEADS):
        q = rope(qk_ref[0, :, h * 128:(h + 1) * 128].astype(F32)) * (DA_DH ** -0.5 * LOG2E)
        k = rope(qk_ref[0, :, BR_W + h * 128:BR_W + (h + 1) * 128].astype(F32))
        k_ref[0, :, h * 128:(h + 1) * 128] = k.astype(BF16)
        qt_ref[0, 2 * h] = jnp.where(first, q, 0.0).T.astype(BF16)
        qt_ref[0, 2 * h + 1] = jnp.where(first, 0.0, q).T.astype(BF16)
        vt_ref[0, 0, h * 128:(h + 1) * 128, :] = v_ref[0, :, h * 128:(h + 1) * 128].astype(F32).T.astype(BF16)


def _attn_prep(p, cos_t, sin_t):
    nb, nt, _ = p.shape
    tm = ROW_TILE
    nti = nt // tm
    return pl.pallas_call(
        _aprep_kernel,
        grid=(nb, nti),
        in_specs=[pl.BlockSpec((1, tm, 2 * BR_W), lambda b, i: (b, i, 0)),
                  pl.BlockSpec((1, tm, BR_W), lambda b, i: (b, i, U_AV)),
                  pl.BlockSpec((tm, 128), lambda b, i: (i, 0)),
                  pl.BlockSpec((tm, 128), lambda b, i: (i, 0))],
        out_specs=[pl.BlockSpec((1, 2 * DA_HEADS, 128, tm), lambda b, i: (b, 0, 0, i)),
                   pl.BlockSpec((1, tm, BR_W), lambda b, i: (b, i, 0)),
                   pl.BlockSpec((1, 1, BR_W, tm), lambda b, i: (b, i, 0, 0))],
        out_shape=[jax.ShapeDtypeStruct((nb, 2 * DA_HEADS, 128, nt), BF16),
                   jax.ShapeDtypeStruct((nb, nt, BR_W), BF16),
                   jax.ShapeDtypeStruct((nb, nti, BR_W, tm), BF16)],
        compiler_params=_cparams("parallel", "parallel"),
        name="attn_prep",
    )(p, p, cos_t, sin_t)


def _lambda(lq_ref, lk_ref, lam_init):
    e = jnp.exp(jnp.sum(lq_ref[...] * lk_ref[...], axis=-1, keepdims=True))
    return e[0:1] - e[1:2] + lam_init


SUM_ROWS = 16


def _attn_kernel(lq_ref, lk_ref, g_ref, q1_ref, q2_ref, k_ref, vt_ref, z_ref, o_ref, acc1, acc2,
                 s1a, s1b, s2a, s2b, *, nkv, tk, lam_init):
    q1 = q1_ref[0, 0]
    q2 = q2_ref[0, 0]
    tq = q1.shape[1]
    acc1[...] = jnp.zeros_like(acc1)
    acc2[...] = jnp.zeros_like(acc2)
    ones = jnp.ones((SUM_ROWS, tk), BF16)

    def scores(q, s_scr, j):
        kk = k_ref[0, pl.ds(pl.multiple_of(j * tk, tk), tk), :]
        s = jnp.dot(kk, q, preferred_element_type=F32)
        s_scr[...] = s
        return jnp.max(s, axis=0, keepdims=True)

    def accumulate(s_scr, acc, j, mx, m):
        vv = jnp.concatenate([vt_ref[0, j], ones], axis=0)
        mn = jnp.maximum(m, mx)
        p = jnp.exp2(s_scr[...] - mn).astype(BF16)
        acc[...] = acc[...] * jnp.exp2(m - mn) + jnp.dot(vv, p, preferred_element_type=F32)
        return mn

    def stage(j, bufs_cur, bufs_next, carry):
        mx1, mx2, m1, m2 = carry
        nmx1 = scores(q1, bufs_next[0], j + 1)
        m1 = accumulate(bufs_cur[0], acc1, j, mx1, m1)
        nmx2 = scores(q2, bufs_next[1], j + 1)
        m2 = accumulate(bufs_cur[1], acc2, j, mx2, m2)
        return nmx1, nmx2, m1, m2

    def body(jj, carry):
        carry = stage(2 * jj, (s1a, s2a), (s1b, s2b), carry)
        return stage(2 * jj + 1, (s1b, s2b), (s1a, s2a), carry)

    neg = jnp.full((1, tq), NEG, F32)
    carry = (scores(q1, s1a, 0), scores(q2, s2a, 0), neg, neg)
    carry = lax.fori_loop(0, (nkv - 1) // 2, body, carry)
    if (nkv - 1) % 2:
        carry = stage(nkv - 2, (s1a, s2a), (s1b, s2b), carry)
        last_bufs = (s1b, s2b)
    else:
        last_bufs = (s1a, s2a)
    mx1, mx2, m1, m2 = carry
    accumulate(last_bufs[0], acc1, nkv - 1, mx1, m1)
    accumulate(last_bufs[1], acc2, nkv - 1, mx2, m2)

    lam = _lambda(lq_ref, lk_ref, lam_init)
    l1 = acc1[DA_DV:DA_DV + 1, :]
    l2 = acc2[DA_DV:DA_DV + 1, :]
    o = acc1[0:DA_DV, :] / l1 - lam * (acc2[0:DA_DV, :] / l2)
    ms = jnp.mean(o * o, axis=0, keepdims=True)
    y = (o * lax.rsqrt(ms + EPS)).T * g_ref[...] * (1.0 - lam_init)
    o_ref[0] = (y * _silu(z_ref[0].astype(F32))).astype(o_ref.dtype)


def _diff_attention(p, qt, kr, vt, lam_q, lam_k, da_g, lam_init, n_lat, tq):
    nb, nt, _ = p.shape
    nkv, tk = vt.shape[1], vt.shape[3]
    return pl.pallas_call(
        functools.partial(_attn_kernel, nkv=nkv, tk=tk, lam_init=lam_init),
        grid=(nb, DA_HEADS, n_lat // tq),
        in_specs=[pl.BlockSpec((2, DA_DH), lambda b, h, i: (0, 0)),
                  pl.BlockSpec((2, DA_DH), lambda b, h, i: (0, 0)),
                  pl.BlockSpec((1, DA_DV), lambda b, h, i: (0, 0)),
                  pl.BlockSpec((1, 1, 128, tq), lambda b, h, i: (b, 2 * h, 0, i)),
                  pl.BlockSpec((1, 1, 128, tq), lambda b, h, i: (b, 2 * h + 1, 0, i)),
                  pl.BlockSpec((1, nt, 128), lambda b, h, i: (b, 0, h)),
                  pl.BlockSpec((1, nkv, 128, tk), lambda b, h, i: (b, 0, h, 0)),
                  pl.BlockSpec((1, tq, 128), lambda b, h, i: (b, i, U_AZ * 4 + h))],
        out_specs=pl.BlockSpec((1, tq, 128), lambda b, h, i: (b, i, h)),
        out_shape=jax.ShapeDtypeStruct((nb, n_lat, BR_W), BF16),
        scratch_shapes=[pltpu.VMEM((DA_DV + SUM_ROWS, tq), F32), pltpu.VMEM((DA_DV + SUM_ROWS, tq), F32),
                        pltpu.VMEM((tk, tq), F32), pltpu.VMEM((tk, tq), F32),
                        pltpu.VMEM((tk, tq), F32), pltpu.VMEM((tk, tq), F32)],
        compiler_params=_cparams("parallel", "parallel", "parallel"),
        name="diff_attention",
    )(lam_q, lam_k, da_g.reshape(1, DA_DV), qt, qt, kr, vt, p)


HALO = 16


def _log_sigmoid(x):
    return jnp.minimum(x, 0.0) - jnp.log1p(jnp.exp(-jnp.abs(x)))


def _bprep_kernel(x_ref, prev_ref, next_ref, g_ref, w_ref, cb_ref, gb_ref, qk_ref, gl_ref, *, n_lat, n_tok, tm):
    i = pl.program_id(1)
    x = x_ref[0].astype(F32)
    row = lax.broadcasted_iota(jnp.int32, (tm, 1), 0)
    pos = i * tm + row
    xp = jnp.where(row == 0, prev_ref[0, HALO - 1:HALO, :].astype(F32), pltpu.roll(x, 1, 0))
    xp = jnp.where((pos == 0) | (pos == n_lat), 0.0, xp)
    xn = jnp.where(row == tm - 1, next_ref[0, 0:1, :].astype(F32), pltpu.roll(x, tm - 1, 0))
    xn = jnp.where((pos == n_lat - 1) | (pos == n_tok - 1), 0.0, xn)
    y = xp * w_ref[0:1, :] + x * w_ref[1:2, :] + xn * w_ref[2:3, :] + cb_ref[...]
    y = _silu(y)
    lane = lax.broadcasted_iota(jnp.int32, (1, 2 * BR_W), 1)
    qk_ref[0] = jnp.where(lane >= BR_W, y * (ML_DH ** -0.5), y).astype(BF16)
    g = g_ref[0] + gb_ref[...]
    gl_lane = lax.broadcasted_iota(jnp.int32, (1, 128), 1)
    is_f = (gl_lane % (2 * ML_HEADS)) >= ML_HEADS
    gl_ref[0] = jnp.where(is_f, _log_sigmoid(g), g)


def _mlstm_prep(p, gates, conv_w, conv_b, gate_b, n_lat):
    nb, nt, _ = p.shape
    tm = ROW_TILE
    hb = tm // HALO
    last = nt // HALO - 1
    gb = jnp.zeros((1, 128), F32).at[0, :N_GATE].set(gate_b.reshape(-1))
    return pl.pallas_call(
        functools.partial(_bprep_kernel, n_lat=n_lat, n_tok=nt, tm=tm),
        grid=(nb, nt // tm),
        in_specs=[pl.BlockSpec((1, tm, 2 * BR_W), lambda b, i: (b, i, U_BQ // 2)),
                  pl.BlockSpec((1, HALO, 2 * BR_W), lambda b, i: (b, jnp.maximum(i * hb - 1, 0), U_BQ // 2)),
                  pl.BlockSpec((1, HALO, 2 * BR_W), lambda b, i: (b, jnp.minimum((i + 1) * hb, last), U_BQ // 2)),
                  pl.BlockSpec((1, tm, 128), lambda b, i: (b, i, 0)),
                  pl.BlockSpec((3, 2 * BR_W), lambda b, i: (0, 0)),
                  pl.BlockSpec((1, 2 * BR_W), lambda b, i: (0, 0)),
                  pl.BlockSpec((1, 128), lambda b, i: (0, 0))],
        out_specs=[pl.BlockSpec((1, tm, 2 * BR_W), lambda b, i: (b, i, 0)),
                   pl.BlockSpec((1, tm, 128), lambda b, i: (b, i, 0))],
        out_shape=[jax.ShapeDtypeStruct((nb, nt, 2 * BR_W), BF16),
                   jax.ShapeDtypeStruct((nb, nt, 128), F32)],
        compiler_params=_cparams("parallel", "parallel"),
        name="mlstm_prep",
    )(p, p, p, gates, conv_w, conv_b.reshape(1, -1), gb)


def _split_dot(a, b_f32, a_is_tri):
    hi = b_f32.astype(BF16)
    lo = (b_f32 - hi.astype(F32)).astype(BF16)
    if a_is_tri:
        return jnp.dot(a, hi, preferred_element_type=F32) + jnp.dot(a, lo, preferred_element_type=F32)
    return jnp.dot(hi, a, preferred_element_type=F32) + jnp.dot(lo, a, preferred_element_type=F32)


def _scan_kernel(qkf_ref, qkb_ref, vf_ref, vb_ref, glf_ref, glb_ref, gtf_ref, gtb_ref,
                 hf_ref, hb_ref, c_scr, n_scr, m_scr):
    t = pl.program_id(1)
    L = ML_CHUNK

    @pl.when(t == 0)
    def _():
        c_scr[...] = jnp.zeros_like(c_scr)
        n_scr[...] = jnp.zeros_like(n_scr)
        m_scr[...] = jnp.zeros_like(m_scr)

    ti = lax.broadcasted_iota(jnp.int32, (L, L), 0)
    si = lax.broadcasted_iota(jnp.int32, (L, L), 1)
    for d in range(2):
        qk_ref, v_ref, gl_ref, gt_ref, h_ref = ((qkf_ref, vf_ref, glf_ref, gtf_ref, hf_ref) if d == 0 else
                                                (qkb_ref, vb_ref, glb_ref, gtb_ref, hb_ref))
        seen = (si <= ti) if d == 0 else (si >= ti)
        tri = seen.astype(BF16)
        tri_t = ((ti <= si) if d == 0 else (ti >= si)).astype(BF16)
        gl = gl_ref[0]
        gt = gt_ref[0, 0]
        bc_all = _split_dot(tri, gl, True)
        br_all = _split_dot(tri_t, gt, False)
        last = L - 1 if d == 0 else 0
        for h in range(ML_HEADS):
            ci = d * 2 * ML_HEADS + h
            cf = ci + ML_HEADS
            idx = d * ML_HEADS + h
            q = qk_ref[0, :, h * ML_DH:(h + 1) * ML_DH]
            k = qk_ref[0, :, BR_W + h * ML_DH:BR_W + (h + 1) * ML_DH]
            v = v_ref[0, :, h * ML_DH:(h + 1) * ML_DH]
            li_c = gl[:, ci:ci + 1]
            b_c = bc_all[:, cf:cf + 1]
            li_r = gt[ci:ci + 1, :]
            b_r = br_all[cf:cf + 1, :]
            b_tot = b_c[last:last + 1, :]
            cmat = c_scr[idx]
            nvec = n_scr[idx]
            m = m_scr[idx][:, 0:1]

            g_c = b_tot - b_c + li_c
            m_new = jnp.maximum(b_tot + m, jnp.max(g_c, axis=0, keepdims=True))
            w_c = jnp.exp(g_c - m_new)
            decay = jnp.exp(b_tot + m - m_new)
            wk = k.astype(F32) * w_c
            c_new = decay * cmat + jnp.dot(wk.T.astype(BF16), v, preferred_element_type=F32)
            n_new = decay * nvec + jnp.sum(wk, axis=0, keepdims=True)

            dmat = jnp.where(seen, b_c - b_r + li_r, NEG)
            m_inter = b_c + m
            m_t = jnp.maximum(m_inter, jnp.max(dmat, axis=-1, keepdims=True))
            a_inter = jnp.exp(m_inter - m_t)
            s = lax.dot_general(q, k, (((1,), (1,)), ((), ())), preferred_element_type=F32)
            s = s * jnp.exp(dmat - m_t)
            num = a_inter * jnp.dot(q, cmat.astype(BF16), preferred_element_type=F32) \
                + jnp.dot(s.astype(BF16), v, preferred_element_type=F32)
            den = a_inter * jnp.sum(q.astype(F32) * nvec, axis=-1, keepdims=True) \
                + jnp.sum(s, axis=-1, keepdims=True)
            hout = num / jnp.maximum(jnp.abs(den), jnp.exp(-m_t))
            h_ref[0, :, h * ML_DH:(h + 1) * ML_DH] = hout

            c_scr[idx] = c_new
            n_scr[idx] = n_new
            m_scr[idx] = jnp.broadcast_to(m_new, (1, 128))


def _mlstm_scan(p, qkc, gl, glt, n_lat):
    nb, nt, _ = p.shape
    L = ML_CHUNK
    nch = nt // L
    nlc = n_lat // L
    fwd = lambda t: (t + nlc) % nch
    bwd = lambda t: nch - 1 - t
    return pl.pallas_call(
        _scan_kernel,
        grid=(nb, nch),
        in_specs=[pl.BlockSpec((1, L, 2 * BR_W), lambda b, t: (b, fwd(t), 0)),
                  pl.BlockSpec((1, L, 2 * BR_W), lambda b, t: (b, bwd(t), 0)),
                  pl.BlockSpec((1, L, BR_W), lambda b, t: (b, fwd(t), U_BV)),
                  pl.BlockSpec((1, L, BR_W), lambda b, t: (b, bwd(t), U_BV)),
                  pl.BlockSpec((1, L, 128), lambda b, t: (b, fwd(t), 0)),
                  pl.BlockSpec((1, L, 128), lambda b, t: (b, bwd(t), 0)),
                  pl.BlockSpec((1, 1, N_GATE, L), lambda b, t: (b, fwd(t), 0, 0)),
                  pl.BlockSpec((1, 1, N_GATE, L), lambda b, t: (b, bwd(t), 0, 0))],
        out_specs=[pl.BlockSpec((1, L, BR_W), lambda b, t: (b, fwd(t), 0)),
                   pl.BlockSpec((1, L, BR_W), lambda b, t: (b, bwd(t), 0))],
        out_shape=[jax.ShapeDtypeStruct((nb, nt, BR_W), F32),
                   jax.ShapeDtypeStruct((nb, nt, BR_W), F32)],
        scratch_shapes=[pltpu.VMEM((2 * ML_HEADS, ML_DH, ML_DH), F32),
                        pltpu.VMEM((2 * ML_HEADS, 1, ML_DH), F32),
                        pltpu.VMEM((2 * ML_HEADS, 1, 128), F32)],
        compiler_params=_cparams("parallel", "arbitrary"),
        name="mlstm_scan",
    )(qkc, qkc, p, p, gl, gl, glt, glt)


NA_RB = 8
NA_BLK = NA_RB * GRID_W


def _na_bias_table(rpb):
    c = np.arange(GRID_W)
    kcol = np.arange(GRID_W)
    cs = np.clip(c - NA_KW // 2, 0, GRID_W - NA_KW)
    ok = (kcol[None, :] >= cs[:, None]) & (kcol[None, :] < cs[:, None] + NA_KW)
    pad = GRID_W - NA_KW
    rp = jnp.pad(rpb.astype(F32), ((0, 0), (0, 0), (pad, pad)))
    toep = jnp.stack([rp[:, :, pad + NA_KW - 1 - ci:pad + NA_KW - 1 - ci + GRID_W] for ci in range(GRID_W)],
                     axis=2)
    toep = jnp.where(jnp.asarray(ok)[None, None], toep, NEG)
    tab = jnp.stack([toep[:, j0:j0 + NA_KH] for j0 in range(NA_KH)], axis=0)
    tab = jnp.transpose(tab, (0, 1, 3, 2, 4))
    return tab.reshape(NA_KH, NA_HEADS, GRID_W, NA_KH * GRID_W)


def _na_kernel(q_ref, kp_ref, kc_ref, kn_ref, vp_ref, vc_ref, vn_ref, kxs, vxs, z_ref, bias_ref,
               o_ref, ks, vs, *, rows):
    i = pl.program_id(1)
    nb = NA_BLK
    ks[0:nb] = kp_ref[0]
    ks[nb:2 * nb] = kc_ref[0]
    ks[2 * nb:3 * nb] = kn_ref[0]
    vs[0:nb] = vp_ref[0]
    vs[nb:2 * nb] = vc_ref[0]
    vs[2 * nb:3 * nb] = vn_ref[0]
    first = lax.broadcasted_iota(jnp.int32, (1, 128), 1) < NA_DH
    nloc = NA_KH * GRID_W
    nt_dims = (((1,), (1,)), ((), ()))

    def row_body(rr, carry):
        r = i * NA_RB + rr
        rs = jnp.clip(r - NA_KH // 2, 0, rows - NA_KH)
        off = pl.multiple_of((rs - (i - 1) * NA_RB) * GRID_W, GRID_W)
        j0 = rs - r + NA_KH - 1
        qoff = pl.multiple_of(rr * GRID_W, GRID_W)
        qr = q_ref[0, pl.ds(qoff, GRID_W), :].astype(F32) * (NA_DH ** -0.5)
        outs = []
        for pr in range(NA_HEADS // 2):
            cols = slice(pr * 128, (pr + 1) * 128)
            qp = qr[:, cols]
            kl = ks[pl.ds(off, nloc), cols]
            vl = vs[pl.ds(off, nloc), cols]
            kx = kxs[0, :, cols]
            vx = vxs[0, :, cols]
            pair = []
            for hh in range(2):
                qm = jnp.where(first if hh == 0 else jnp.logical_not(first), qp, 0.0).astype(BF16)
                s_loc = lax.dot_general(qm, kl, nt_dims, preferred_element_type=F32) + bias_ref[j0, 2 * pr + hh]
                s_ctx = lax.dot_general(qm, kx, nt_dims, preferred_element_type=F32)
                m = jnp.maximum(jnp.max(s_loc, axis=-1, keepdims=True), jnp.max(s_ctx, axis=-1, keepdims=True))
                p_loc = jnp.exp(s_loc - m)
                p_ctx = jnp.exp(s_ctx - m)
                l = jnp.sum(p_loc, axis=-1, keepdims=True) + jnp.sum(p_ctx, axis=-1, keepdims=True)
                o = jnp.dot(p_loc.astype(BF16), vl, preferred_element_type=F32) \
                    + jnp.dot(p_ctx.astype(BF16), vx, preferred_element_type=F32)
                pair.append(o / l)
            outs.append(jnp.where(first, pair[0], pair[1]))
        o_all = jnp.concatenate(outs, axis=-1)
        z = z_ref[0, pl.ds(qoff, GRID_W), :].astype(F32)
        o_ref[0, pl.ds(qoff, GRID_W), :] = (o_all * _silu(z)).astype(o_ref.dtype)
        return carry

    lax.fori_loop(0, NA_RB, row_body, 0)


def _neighbourhood_attention(p, bias_tab, n_lat, n_ctx):
    nb, nt, _ = p.shape
    rows = n_lat // GRID_W
    nblk = rows // NA_RB
    blk = NA_BLK
    cblk = n_lat // n_ctx
    spec = lambda unit, f: pl.BlockSpec((1, blk, BR_W), lambda b, i: (b, f(i), unit))
    prev = lambda i: jnp.maximum(i - 1, 0)
    cur = lambda i: i
    nxt = lambda i: jnp.minimum(i + 1, nblk - 1)
    return pl.pallas_call(
        functools.partial(_na_kernel, rows=rows),
        grid=(nb, nblk),
        in_specs=[spec(U_CQ, cur),
                  spec(U_CK, prev), spec(U_CK, cur), spec(U_CK, nxt),
                  spec(U_CV, prev), spec(U_CV, cur), spec(U_CV, nxt),
                  pl.BlockSpec((1, n_ctx, BR_W), lambda b, i: (b, cblk, U_CK)),
                  pl.BlockSpec((1, n_ctx, BR_W), lambda b, i: (b, cblk, U_CV)),
                  spec(U_CZ, cur),
                  pl.BlockSpec((NA_KH, NA_HEADS, GRID_W, NA_KH * GRID_W), lambda b, i: (0, 0, 0, 0))],
        out_specs=pl.BlockSpec((1, blk, BR_W), lambda b, i: (b, i, 0)),
        out_shape=jax.ShapeDtypeStruct((nb, n_lat, BR_W), BF16),
        scratch_shapes=[pltpu.VMEM((3 * blk, BR_W), BF16), pltpu.VMEM((3 * blk, BR_W), BF16)],
        compiler_params=_cparams("parallel", "parallel"),
        name="neighbourhood_attention",
    )(p, p, p, p, p, p, p, p, p, p, bias_tab)


def _ctx_kernel(lq_ref, lk_ref, g_ref, aqk_ref, avz_ref, cqk_ref, cvz_ref, ya_ref, yc_ref, *, lam_init):
    lam = _lambda(lq_ref, lk_ref, lam_init)
    first = lax.broadcasted_iota(jnp.int32, (1, 128), 1) < 64
    second = jnp.logical_not(first)
    nt_dims = (((1,), (1,)), ((), ()))

    def softmax(s):
        e = jnp.exp(s - jnp.max(s, axis=-1, keepdims=True))
        return e / jnp.sum(e, axis=-1, keepdims=True)

    def masked_scores(q, k, msk):
        return lax.dot_general(jnp.where(msk, q, 0.0).astype(BF16), k, nt_dims, preferred_element_type=F32)

    outs = []
    for h in range(DA_HEADS):
        cols = slice(h * 128, (h + 1) * 128)
        q = aqk_ref[0, :, cols].astype(F32) * (DA_DH ** -0.5)
        k = aqk_ref[0, :, BR_W + h * 128:BR_W + (h + 1) * 128]
        v = avz_ref[0, :, cols]
        z = avz_ref[0, :, BR_W + h * 128:BR_W + (h + 1) * 128].astype(F32)
        a = softmax(masked_scores(q, k, first)) - lam * softmax(masked_scores(q, k, second))
        o = jnp.dot(a.astype(BF16), v, preferred_element_type=F32)
        ms = jnp.mean(o * o, axis=-1, keepdims=True)
        y = o * lax.rsqrt(ms + EPS) * g_ref[...] * (1.0 - lam_init)
        outs.append(y * _silu(z))
    ya_ref[0] = jnp.concatenate(outs, axis=-1).astype(ya_ref.dtype)

    outs = []
    for pr in range(NA_HEADS // 2):
        cols = slice(pr * 128, (pr + 1) * 128)
        q = cqk_ref[0, :, cols].astype(F32) * (NA_DH ** -0.5)
        k = cqk_ref[0, :, BR_W + pr * 128:BR_W + (pr + 1) * 128]
        v = cvz_ref[0, :, cols]
        pair = [jnp.dot(softmax(masked_scores(q, k, msk)).astype(BF16), v, preferred_element_type=F32)
                for msk in (first, second)]
        outs.append(jnp.where(first, pair[0], pair[1]))
    z = cvz_ref[0, :, BR_W:].astype(F32)
    yc_ref[0] = (jnp.concatenate(outs, axis=-1) * _silu(z)).astype(yc_ref.dtype)


def _ctx_attention(p, lam_q, lam_k, da_g, lam_init, n_lat, n_ctx):
    nb = p.shape[0]
    cblk = n_lat // n_ctx
    pspec = lambda unit: pl.BlockSpec((1, n_ctx, 2 * BR_W), lambda b: (b, cblk, unit // 2))
    yspec = pl.BlockSpec((1, n_ctx, BR_W), lambda b: (b, 0, 0))
    small = lambda shape: pl.BlockSpec(shape, lambda b: (0, 0))
    yshape = jax.ShapeDtypeStruct((nb, n_ctx, BR_W), BF16)
    return pl.pallas_call(
        functools.partial(_ctx_kernel, lam_init=lam_init),
        grid=(nb,),
        in_specs=[small((2, DA_DH)), small((2, DA_DH)), small((1, DA_DV)),
                  pspec(U_AQ), pspec(U_AV), pspec(U_CQ), pspec(U_CV)],
        out_specs=[yspec, yspec],
        out_shape=[yshape, yshape],
        compiler_params=_cparams("parallel"),
        name="ctx_attention",
    )(lam_q, lam_k, da_g.reshape(1, DA_DV), p, p, p, p)


def _merge_kernel(x_ref, ya_ref, yc_ref, hf_ref, hb_ref, bo_ref, bz_ref, gm0_ref, gm1_ref, gm2_ref,
                  mod_ref, mlg_ref, fg_ref, wbr_ref, wout_ref, o_ref, *, n_batch, ctx_rows, final_norm):
    b = pl.program_id(0)
    hsum = hf_ref[0] + hb_ref[0]
    parts = []
    for h in range(ML_HEADS):
        hh = hsum[:, h * ML_DH:(h + 1) * ML_DH]
        ms = jnp.mean(hh * hh, axis=-1, keepdims=True)
        parts.append(hh * lax.rsqrt(ms + EPS) * mlg_ref[...])
    hn = jnp.concatenate(parts, axis=-1)
    yb = (jax.nn.sigmoid(bo_ref[0].astype(F32)) * hn * _silu(bz_ref[0].astype(F32))).astype(BF16)

    def gated(gm_ref, y, n):
        return jax.nn.sigmoid(gm_ref[0].astype(F32)) * jnp.dot(y, wbr_ref[n], preferred_element_type=F32)

    merged = gated(gm0_ref, ya_ref[0], 0) + gated(gm1_ref, yb, 1) + gated(gm2_ref, yc_ref[0], 2)
    upd = jnp.dot(merged.astype(BF16), wout_ref[...], preferred_element_type=F32)

    gate = mod_ref[n_batch:n_batch + 1, 2 * D_MODEL:] if ctx_rows else mod_ref[pl.ds(b, 1), 2 * D_MODEL:]
    xo = x_ref[0] + gate * upd
    if final_norm:
        ms = jnp.mean(xo * xo, axis=-1, keepdims=True)
        xo = xo * lax.rsqrt(ms + EPS) * fg_ref[...]
    o_ref[0] = xo


def _merge(xa, p, ya, yc, hf, hb, mod, ml_g, final_g, wbr, wout, n_rows, tm, row_off, ctx_rows, final_norm):
    nb, _, d = xa.shape
    own = lambda width: pl.BlockSpec((1, tm, width), lambda b, i: (b, i, 0))
    row = lambda width, unit: pl.BlockSpec((1, tm, width), lambda b, i: (b, i + row_off, unit))
    const = lambda shape: pl.BlockSpec(shape, lambda b, i: (0,) * len(shape))
    return pl.pallas_call(
        functools.partial(_merge_kernel, n_batch=nb, ctx_rows=ctx_rows, final_norm=final_norm),
        grid=(nb, n_rows // tm),
        in_specs=[row(d, 0), own(BR_W), own(BR_W), row(BR_W, 0), row(BR_W, 0),
                  row(BR_W, U_BO), row(BR_W, U_BZ),
                  row(d, U_GM // 2), row(d, U_GM // 2 + 1), row(d, U_GM // 2 + 2),
                  const((8, 3 * d)), const((1, ML_DH)), const((1, d)),
                  const((3, BR_W, d)), const((d, d))],
        out_specs=own(d),
        out_shape=jax.ShapeDtypeStruct((nb, n_rows, d), F32),
        compiler_params=_cparams("parallel", "parallel"),
        name="merge",
    )(xa, ya, yc, hf, hb, p, p, p, p, p, mod, ml_g.reshape(1, ML_DH), final_g.reshape(1, d), wbr, wout)


def _rope_tables(n_lat, n_tok):
    t = jnp.arange(n_lat, dtype=jnp.int32)
    row = (t // GRID_W).astype(F32)
    col = (t % GRID_W).astype(F32)
    inv = ROPE_BASE ** (-jnp.arange(0, DA_DH // 2, 2, dtype=F32) / (DA_DH // 2))
    ang = jnp.concatenate([row[:, None] * inv, col[:, None] * inv], axis=-1)
    cos = jnp.repeat(jnp.cos(ang), 2, axis=-1)
    sin = jnp.repeat(jnp.sin(ang), 2, axis=-1) * jnp.tile(jnp.asarray([-1.0, 1.0], F32), DA_DH // 2)
    cos = jnp.tile(cos, (1, 2))
    sin = jnp.tile(sin, (1, 2))
    cos = jnp.concatenate([cos, jnp.ones((n_tok - n_lat, 128), F32)], axis=0)
    sin = jnp.concatenate([sin, jnp.zeros((n_tok - n_lat, 128), F32)], axis=0)
    return cos, sin


def _pad_w_in(w):
    d = w.shape[0]
    split = 9 * BR_W + N_GATE
    return jnp.concatenate([w[:, :split], jnp.zeros((d, GATE_PAD - N_GATE), w.dtype), w[:, split:]], axis=1).astype(BF16)


def kernel(x, c, ctx, c_ctx, w_mod, b_mod, norm_g, w_in, da_lam_q, da_lam_k, da_norm_g, ml_conv_w, ml_conv_b,
           ml_gate_b, ml_norm_g, na_rpb, w_br, w_out, final_g):
    nb, n_lat, d = x.shape
    n_ctx = ctx.shape[1]
    nt = n_lat + n_ctx
    depth = w_mod.shape[0]
    assert d == D_MODEL and nt % ROW_TILE == 0 and n_lat % n_ctx == 0 and n_lat % NA_BLK == 0 and nb + 1 <= 8
    tq = 512

    cs = jnp.zeros((8, d), F32).at[:nb].set(c).at[nb].set(c_ctx)
    mods = _modulation(cs, w_mod, b_mod)
    cos_t, sin_t = _rope_tables(n_lat, nt)
    xa = jnp.concatenate([x, ctx], axis=1)

    for l in range(depth):
        last = l == depth - 1
        lam_init = 0.8 - 0.6 * math.exp(-0.3 * l)
        p, gates = _projection(xa, mods[l], norm_g[l], _pad_w_in(w_in[l]), n_lat)

        qt, kr, vt = _attn_prep(p, cos_t, sin_t)
        ya = _diff_attention(p, qt, kr, vt, da_lam_q[l], da_lam_k[l], da_norm_g[l], lam_init, n_lat, tq)

        qkc, gl = _mlstm_prep(p, gates, ml_conv_w[l], ml_conv_b[l], ml_gate_b[l], n_lat)
        glt = jnp.transpose(gl[:, :, :N_GATE].reshape(nb, nt // ML_CHUNK, ML_CHUNK, N_GATE), (0, 1, 3, 2))
        hf, hb = _mlstm_scan(p, qkc, gl, glt, n_lat)

        yc = _neighbourhood_attention(p, _na_bias_table(na_rpb[l]), n_lat, n_ctx)

        wbr = w_br[l].astype(BF16)
        wout = w_out[l].astype(BF16)
        merge = functools.partial(_merge, xa, p, hf=hf, hb=hb, mod=mods[l], ml_g=ml_norm_g[l], final_g=final_g,
                                  wbr=wbr, wout=wout)
        x_lat = merge(ya=ya, yc=yc, n_rows=n_lat, tm=512, row_off=0, ctx_rows=False, final_norm=last)
        if last:
            return x_lat
        ya_c, yc_c = _ctx_attention(p, da_lam_q[l], da_lam_k[l], da_norm_g[l], lam_init, n_lat, n_ctx)
        x_ctx = merge(ya=ya_c, yc=yc_c, n_rows=n_ctx, tm=n_ctx, row_off=n_lat // n_ctx, ctx_rows=True,
                      final_norm=False)
        xa = jnp.concatenate([x_lat, x_ctx], axis=1)
```

```python
import functools
import math

import numpy as np
import jax
import jax.numpy as jnp
from jax import lax
from jax.experimental import pallas as pl
from jax.experimental.pallas import tpu as pltpu

F32 = jnp.float32
BF16 = jnp.bfloat16

D_MODEL = 1024
BR_W = 512
GRID_W = 64
EPS = 1e-6
ROPE_BASE = 10000.0
DA_HEADS, DA_DH, DA_DV = 4, 64, 128
ML_HEADS, ML_DH, ML_CHUNK = 4, 128, 64
NA_HEADS, NA_DH, NA_KH, NA_KW = 8, 64, 8, 16
N_GATE = 4 * ML_HEADS
GATE_PAD = 512
P_W = 9 * BR_W + GATE_PAD + 4 * BR_W + 3 * D_MODEL
U_AQ, U_AK, U_AV, U_AZ, U_BQ, U_BK, U_BV, U_BO, U_BZ, U_BG, U_CQ, U_CK, U_CV, U_CZ, U_GM = range(15)
NEG = -1e30
LOG2E = math.log2(math.e)
VMEM_LIMIT = 56 * 1024 * 1024
ROW_TILE = 768


def _cparams(*sem):
    return pltpu.CompilerParams(dimension_semantics=sem, vmem_limit_bytes=VMEM_LIMIT)


def _silu(x):
    return x * jax.nn.sigmoid(x)


def _mod_kernel(c_ref, w_ref, b_ref, o_ref):
    s = _silu(c_ref[...])
    o_ref[0] = jnp.dot(s.astype(BF16), w_ref[0].astype(BF16), preferred_element_type=F32) + b_ref[0]


def _modulation(cs, w_mod, b_mod):
    depth, d, d3 = w_mod.shape
    tn = 1024
    return pl.pallas_call(
        _mod_kernel,
        grid=(depth, d3 // tn),
        in_specs=[pl.BlockSpec((8, d), lambda l, j: (0, 0)),
                  pl.BlockSpec((1, d, tn), lambda l, j: (l, 0, j)),
                  pl.BlockSpec((1, 1, tn), lambda l, j: (l, 0, j))],
        out_specs=pl.BlockSpec((1, 8, tn), lambda l, j: (l, 0, j)),
        out_shape=jax.ShapeDtypeStruct((depth, 8, d3), F32),
        compiler_params=_cparams("parallel", "parallel"),
        name="modulation",
    )(cs, w_mod, b_mod.reshape(depth, 1, d3))


PROJ_TN = 1024
GATE_TILE = (U_BG * BR_W) // PROJ_TN
GATE_OFF = (U_BG * BR_W) % PROJ_TN


def _proj_kernel(x_ref, mod_ref, g_ref, w_ref, o_ref, gate_ref, h_scr, *, n_lat, n_batch, tm):
    b = pl.program_id(0)
    i = pl.program_id(1)
    j = pl.program_id(2)

    @pl.when(j == 0)
    def _():
        x = x_ref[0]
        ms = jnp.mean(x * x, axis=-1, keepdims=True)
        y = x * lax.rsqrt(ms + EPS) * g_ref[...]
        row = i * tm + lax.broadcasted_iota(jnp.int32, (tm, 1), 0)
        is_ctx = row >= n_lat
        mb = mod_ref[pl.ds(b, 1), :]
        mc = mod_ref[n_batch:n_batch + 1, :]
        shift = jnp.where(is_ctx, mc[:, :D_MODEL], mb[:, :D_MODEL])
        scale = jnp.where(is_ctx, mc[:, D_MODEL:2 * D_MODEL], mb[:, D_MODEL:2 * D_MODEL])
        h_scr[...] = (y * (1.0 + scale) + shift).astype(BF16)

    acc = jnp.dot(h_scr[...], w_ref[...], preferred_element_type=F32)
    o_ref[0] = acc.astype(BF16)

    @pl.when(j == GATE_TILE)
    def _():
        gate_ref[0] = acc[:, GATE_OFF:GATE_OFF + 128]


def _projection(xa, mod, norm_g, w_pad, n_lat):
    nb, nt, d = xa.shape
    tm, tn = ROW_TILE, PROJ_TN
    return pl.pallas_call(
        functools.partial(_proj_kernel, n_lat=n_lat, n_batch=nb, tm=tm),
        grid=(nb, nt // tm, P_W // tn),
        in_specs=[pl.BlockSpec((1, tm, d), lambda b, i, j: (b, i, 0)),
                  pl.BlockSpec((8, 3 * d), lambda b, i, j: (0, 0)),
                  pl.BlockSpec((1, d), lambda b, i, j: (0, 0)),
                  pl.BlockSpec((d, tn), lambda b, i, j: (0, j))],
        out_specs=[pl.BlockSpec((1, tm, tn), lambda b, i, j: (b, i, j)),
                   pl.BlockSpec((1, tm, 128), lambda b, i, j: (b, i, 0))],
        out_shape=[jax.ShapeDtypeStruct((nb, nt, P_W), BF16),
                   jax.ShapeDtypeStruct((nb, nt, 128), F32)],
        scratch_shapes=[pltpu.VMEM((tm, d), BF16)],
        compiler_params=_cparams("parallel", "parallel", "arbitrary"),
        name="norm_proj",
    )(xa, mod, norm_g.reshape(1, d), w_pad)


def _aprep_kernel(qk_ref, v_ref, cos_ref, sin_ref, qt_ref, k_ref, vt_ref):
    cosv = cos_ref[...]
    sinv = sin_ref[...]
    lane = lax.broadcasted_iota(jnp.int32, (1, 128), 1)
    even = (lane % 2) == 0
    first = lane < DA_DH

    def rope(x):
        swapped = jnp.where(even, pltpu.roll(x, 127, 1), pltpu.roll(x, 1, 1))
        return x * cosv + swapped * sinv

    for h in range(DA_HEADS):
        q = rope(qk_ref[0, :, h * 128:(h + 1) * 128].astype(F32)) * (DA_DH ** -0.5 * LOG2E)
        k = rope(qk_ref[0, :, BR_W + h * 128:BR_W + (h + 1) * 128].astype(F32))
        k_ref[0, :, h * 128:(h + 1) * 128] = k.astype(BF16)
        qt_ref[0, 2 * h] = jnp.where(first, q, 0.0).T.astype(BF16)
        qt_ref[0, 2 * h + 1] = jnp.where(first, 0.0, q).T.astype(BF16)
        vt_ref[0, 0, h * 128:(h + 1) * 128, :] = v_ref[0, :, h * 128:(h + 1) * 128].astype(F32).T.astype(BF16)


def _attn_prep(p, cos_t, sin_t):
    nb, nt, _ = p.shape
    tm = ROW_TILE
    nti = nt // tm
    return pl.pallas_call(
        _aprep_kernel,
        grid=(nb, nti),
        in_specs=[pl.BlockSpec((1, tm, 2 * BR_W), lambda b, i: (b, i, 0)),
                  pl.BlockSpec((1, tm, BR_W), lambda b, i: (b, i, U_AV)),
                  pl.BlockSpec((tm, 128), lambda b, i: (i, 0)),
                  pl.BlockSpec((tm, 128), lambda b, i: (i, 0))],
        out_specs=[pl.BlockSpec((1, 2 * DA_HEADS, 128, tm), lambda b, i: (b, 0, 0, i)),
                   pl.BlockSpec((1, tm, BR_W), lambda b, i: (b, i, 0)),
                   pl.BlockSpec((1, 1, BR_W, tm), lambda b, i: (b, i, 0, 0))],
        out_shape=[jax.ShapeDtypeStruct((nb, 2 * DA_HEADS, 128, nt), BF16),
                   jax.ShapeDtypeStruct((nb, nt, BR_W), BF16),
                   jax.ShapeDtypeStruct((nb, nti, BR_W, tm), BF16)],
        compiler_params=_cparams("parallel", "parallel"),
        name="attn_prep",
    )(p, p, cos_t, sin_t)


def _lambda(lq_ref, lk_ref, lam_init):
    e = jnp.exp(jnp.sum(lq_ref[...] * lk_ref[...], axis=-1, keepdims=True))
    return e[0:1] - e[1:2] + lam_init


SUM_ROWS = 16


def _attn_kernel(lq_ref, lk_ref, g_ref, q1_ref, q2_ref, k_ref, vt_ref, z_ref, o_ref, acc1, acc2,
                 s1a, s1b, s2a, s2b, *, nkv, tk, lam_init):
    q1 = q1_ref[0, 0]
    q2 = q2_ref[0, 0]
    tq = q1.shape[1]
    acc1[...] = jnp.zeros_like(acc1)
    acc2[...] = jnp.zeros_like(acc2)
    ones = jnp.ones((SUM_ROWS, tk), BF16)

    def scores(q, s_scr, j):
        kk = k_ref[0, pl.ds(pl.multiple_of(j * tk, tk), tk), :]
        s = jnp.dot(kk, q, preferred_element_type=F32)
        s_scr[...] = s
        return jnp.max(s, axis=0, keepdims=True)

    def accumulate(s_scr, acc, j, mx, m):
        vv = jnp.concatenate([vt_ref[0, j], ones], axis=0)
        mn = jnp.maximum(m, mx)
        p = jnp.exp2(s_scr[...] - mn).astype(BF16)
        acc[...] = acc[...] * jnp.exp2(m - mn) + jnp.dot(vv, p, preferred_element_type=F32)
        return mn

    def stage(j, bufs_cur, bufs_next, carry):
        mx1, mx2, m1, m2 = carry
        nmx1 = scores(q1, bufs_next[0], j + 1)
        m1 = accumulate(bufs_cur[0], acc1, j, mx1, m1)
        nmx2 = scores(q2, bufs_next[1], j + 1)
        m2 = accumulate(bufs_cur[1], acc2, j, mx2, m2)
        return nmx1, nmx2, m1, m2

    def body(jj, carry):
        carry = stage(2 * jj, (s1a, s2a), (s1b, s2b), carry)
        return stage(2 * jj + 1, (s1b, s2b), (s1a, s2a), carry)

    neg = jnp.full((1, tq), NEG, F32)
    carry = (scores(q1, s1a, 0), scores(q2, s2a, 0), neg, neg)
    carry = lax.fori_loop(0, (nkv - 1) // 2, body, carry)
    if (nkv - 1) % 2:
        carry = stage(nkv - 2, (s1a, s2a), (s1b, s2b), carry)
        last_bufs = (s1b, s2b)
    else:
        last_bufs = (s1a, s2a)
    mx1, mx2, m1, m2 = carry
    accumulate(last_bufs[0], acc1, nkv - 1, mx1, m1)
    accumulate(last_bufs[1], acc2, nkv - 1, mx2, m2)

    lam = _lambda(lq_ref, lk_ref, lam_init)
    l1 = acc1[DA_DV:DA_DV + 1, :]
    l2 = acc2[DA_DV:DA_DV + 1, :]
    o = acc1[0:DA_DV, :] / l1 - lam * (acc2[0:DA_DV, :] / l2)
    ms = jnp.mean(o * o, axis=0, keepdims=True)
    y = (o * lax.rsqrt(ms + EPS)).T * g_ref[...] * (1.0 - lam_init)
    o_ref[0] = (y * _silu(z_ref[0].astype(F32))).astype(o_ref.dtype)


def _diff_attention(p, qt, kr, vt, lam_q, lam_k, da_g, lam_init, n_lat, tq):
    nb, nt, _ = p.shape
    nkv, tk = vt.shape[1], vt.shape[3]
    return pl.pallas_call(
        functools.partial(_attn_kernel, nkv=nkv, tk=tk, lam_init=lam_init),
        grid=(nb, DA_HEADS, n_lat // tq),
        in_specs=[pl.BlockSpec((2, DA_DH), lambda b, h, i: (0, 0)),
                  pl.BlockSpec((2, DA_DH), lambda b, h, i: (0, 0)),
                  pl.BlockSpec((1, DA_DV), lambda b, h, i: (0, 0)),
                  pl.BlockSpec((1, 1, 128, tq), lambda b, h, i: (b, 2 * h, 0, i)),
                  pl.BlockSpec((1, 1, 128, tq), lambda b, h, i: (b, 2 * h + 1, 0, i)),
                  pl.BlockSpec((1, nt, 128), lambda b, h, i: (b, 0, h)),
                  pl.BlockSpec((1, nkv, 128, tk), lambda b, h, i: (b, 0, h, 0)),
                  pl.BlockSpec((1, tq, 128), lambda b, h, i: (b, i, U_AZ * 4 + h))],
        out_specs=pl.BlockSpec((1, tq, 128), lambda b, h, i: (b, i, h)),
        out_shape=jax.ShapeDtypeStruct((nb, n_lat, BR_W), BF16),
        scratch_shapes=[pltpu.VMEM((DA_DV + SUM_ROWS, tq), F32), pltpu.VMEM((DA_DV + SUM_ROWS, tq), F32),
                        pltpu.VMEM((tk, tq), F32), pltpu.VMEM((tk, tq), F32),
                        pltpu.VMEM((tk, tq), F32), pltpu.VMEM((tk, tq), F32)],
        compiler_params=_cparams("parallel", "parallel", "parallel"),
        name="diff_attention",
    )(lam_q, lam_k, da_g.reshape(1, DA_DV), qt, qt, kr, vt, p)


HALO = 16


def _log_sigmoid(x):
    return jnp.minimum(x, 0.0) - jnp.log1p(jnp.exp(-jnp.abs(x)))


def _bprep_kernel(x_ref, prev_ref, next_ref, v_ref, g_ref, w_ref, cb_ref, gb_ref, k_ref, qt_ref, vt_ref, gl_ref,
                  *, n_lat, n_tok, tm):
    i = pl.program_id(1)
    x = x_ref[0].astype(F32)
    row = lax.broadcasted_iota(jnp.int32, (tm, 1), 0)
    pos = i * tm + row
    xp = jnp.where(row == 0, prev_ref[0, HALO - 1:HALO, :].astype(F32), pltpu.roll(x, 1, 0))
    xp = jnp.where((pos == 0) | (pos == n_lat), 0.0, xp)
    xn = jnp.where(row == tm - 1, next_ref[0, 0:1, :].astype(F32), pltpu.roll(x, tm - 1, 0))
    xn = jnp.where((pos == n_lat - 1) | (pos == n_tok - 1), 0.0, xn)
    y = xp * w_ref[0:1, :] + x * w_ref[1:2, :] + xn * w_ref[2:3, :] + cb_ref[...]
    y = _silu(y)
    k_ref[0] = (y[:, BR_W:] * (ML_DH ** -0.5)).astype(BF16)
    for c in range(tm // ML_CHUNK):
        rows = slice(c * ML_CHUNK, (c + 1) * ML_CHUNK)
        qt_ref[0, c] = y[rows, :BR_W].T.astype(BF16)
        vt_ref[0, c] = v_ref[0, rows, :].astype(F32).T.astype(BF16)
    g = g_ref[0] + gb_ref[...]
    gl_lane = lax.broadcasted_iota(jnp.int32, (1, 128), 1)
    is_f = (gl_lane % (2 * ML_HEADS)) >= ML_HEADS
    gl_ref[0] = jnp.where(is_f, _log_sigmoid(g), g)


def _mlstm_prep(p, gates, conv_w, conv_b, gate_b, n_lat):
    nb, nt, _ = p.shape
    tm = ROW_TILE
    hb = tm // HALO
    last = nt // HALO - 1
    gb = jnp.zeros((1, 128), F32).at[0, :N_GATE].set(gate_b.reshape(-1))
    cpt = tm // ML_CHUNK
    tspec = pl.BlockSpec((1, cpt, BR_W, ML_CHUNK), lambda b, i: (b, i, 0, 0))
    tshape = jax.ShapeDtypeStruct((nb, nt // ML_CHUNK, BR_W, ML_CHUNK), BF16)
    return pl.pallas_call(
        functools.partial(_bprep_kernel, n_lat=n_lat, n_tok=nt, tm=tm),
        grid=(nb, nt // tm),
        in_specs=[pl.BlockSpec((1, tm, 2 * BR_W), lambda b, i: (b, i, U_BQ // 2)),
                  pl.BlockSpec((1, HALO, 2 * BR_W), lambda b, i: (b, jnp.maximum(i * hb - 1, 0), U_BQ // 2)),
                  pl.BlockSpec((1, HALO, 2 * BR_W), lambda b, i: (b, jnp.minimum((i + 1) * hb, last), U_BQ // 2)),
                  pl.BlockSpec((1, tm, BR_W), lambda b, i: (b, i, U_BV)),
                  pl.BlockSpec((1, tm, 128), lambda b, i: (b, i, 0)),
                  pl.BlockSpec((3, 2 * BR_W), lambda b, i: (0, 0)),
                  pl.BlockSpec((1, 2 * BR_W), lambda b, i: (0, 0)),
                  pl.BlockSpec((1, 128), lambda b, i: (0, 0))],
        out_specs=[pl.BlockSpec((1, tm, BR_W), lambda b, i: (b, i, 0)), tspec, tspec,
                   pl.BlockSpec((1, tm, 128), lambda b, i: (b, i, 0))],
        out_shape=[jax.ShapeDtypeStruct((nb, nt, BR_W), BF16), tshape, tshape,
                   jax.ShapeDtypeStruct((nb, nt, 128), F32)],
        compiler_params=_cparams("parallel", "parallel"),
        name="mlstm_prep",
    )(p, p, p, p, gates, conv_w, conv_b.reshape(1, -1), gb)


def _split_dot(a, b_f32, a_is_tri):
    hi = b_f32.astype(BF16)
    lo = (b_f32 - hi.astype(F32)).astype(BF16)
    if a_is_tri:
        return jnp.dot(a, hi, preferred_element_type=F32) + jnp.dot(a, lo, preferred_element_type=F32)
    return jnp.dot(hi, a, preferred_element_type=F32) + jnp.dot(lo, a, preferred_element_type=F32)


def _scan_kernel(kf_ref, kb_ref, qtf_ref, qtb_ref, vtf_ref, vtb_ref, glf_ref, glb_ref, gtf_ref, gtb_ref,
                 hf_ref, hb_ref, c_scr, m_scr):
    t = pl.program_id(1)
    L = ML_CHUNK

    @pl.when(t == 0)
    def _():
        c_scr[...] = jnp.zeros_like(c_scr)
        m_scr[...] = jnp.zeros_like(m_scr)

    si = lax.broadcasted_iota(jnp.int32, (L, L), 0)
    ti = lax.broadcasted_iota(jnp.int32, (L, L), 1)
    ones_t = jnp.ones((ML_DH, L), BF16)
    dirs = ((kf_ref, qtf_ref, vtf_ref, glf_ref, gtf_ref, hf_ref), (kb_ref, qtb_ref, vtb_ref, glb_ref, gtb_ref, hb_ref))
    chains = [(d, h) for d in range(2) for h in range(ML_HEADS)]
    hsl = lambda h: slice(h * ML_DH, (h + 1) * ML_DH)

    seen, gates = [], []
    for d in range(2):
        sd = (si <= ti) if d == 0 else (si >= ti)
        tri_c = ((ti <= si) if d == 0 else (ti >= si)).astype(BF16)
        gl = dirs[d][3][0]
        gt = dirs[d][4][0, 0]
        seen.append(sd)
        gates.append((gl, _split_dot(tri_c, gl, True), _split_dot(sd.astype(BF16), gt, False)))

    ks = [dirs[d][0][0, :, hsl(h)] for d, h in chains]
    qts = [dirs[d][1][0, 0, hsl(h), :] for d, h in chains]
    cmats = [c_scr[i] for i in range(len(chains))]
    raw = [jnp.dot(k, qt, preferred_element_type=F32) for k, qt in zip(ks, qts)]
    inter = [jnp.dot(c.astype(BF16), qt, preferred_element_type=F32) for c, qt in zip(cmats, qts)]

    vecs, h_loc, kv_loc = [], [], []
    for i, (d, h) in enumerate(chains):
        gl, bc_all, br_all = gates[d]
        ci = d * 2 * ML_HEADS + h
        cf = ci + ML_HEADS
        last = L - 1 if d == 0 else 0
        li_c = gl[:, ci:ci + 1]
        b_c = bc_all[:, cf:cf + 1]
        b_r = br_all[cf:cf + 1, :]
        b_tot = b_c[last:last + 1, :]
        dmat = jnp.where(seen[d], b_r + (li_c - b_c), NEG)
        rmax = jnp.max(dmat, axis=0, keepdims=True)
        s = (raw[i] * jnp.exp(dmat - rmax)).astype(BF16)
        g_c = b_tot - b_c + li_c
        g_max = jnp.max(g_c, axis=0, keepdims=True)
        wk = (ks[i].astype(F32) * jnp.exp(g_c - g_max)).astype(BF16)
        vt_aug = jnp.concatenate([dirs[d][2][0, 0, hsl(h), :], ones_t], axis=0)
        h_loc.append(jnp.dot(vt_aug, s, preferred_element_type=F32))
        kv_loc.append(jnp.dot(vt_aug, wk, preferred_element_type=F32))
        vecs.append((b_r, b_tot, rmax, g_max))

    for i, (d, h) in enumerate(chains):
        b_r, b_tot, rmax, g_max = vecs[i]
        m = m_scr[i][:, 0:1]
        m_new = jnp.maximum(b_tot + m, g_max)
        c_scr[i] = jnp.exp(b_tot + m - m_new) * cmats[i] + jnp.exp(g_max - m_new) * kv_loc[i]
        m_scr[i] = jnp.broadcast_to(m_new, (1, 128))
        m_inter = b_r + m
        m_t = jnp.maximum(m_inter, rmax)
        num = jnp.exp(m_inter - m_t) * inter[i] + jnp.exp(rmax - m_t) * h_loc[i]
        den = num[ML_DH:ML_DH + 1, :]
        dirs[d][5][0, 0, hsl(h), :] = num[:ML_DH, :] / jnp.maximum(jnp.abs(den), jnp.exp(-m_t))


def _mlstm_scan(kc, qt, vt, gl, glt, n_lat):
    nb, nt, _ = kc.shape
    L = ML_CHUNK
    nch = nt // L
    nlc = n_lat // L
    fwd = lambda t: (t + nlc) % nch
    bwd = lambda t: nch - 1 - t
    tspec = lambda f: pl.BlockSpec((1, 1, BR_W, L), lambda b, t: (b, f(t), 0, 0))
    rspec = lambda width, f: pl.BlockSpec((1, L, width), lambda b, t: (b, f(t), 0))
    gspec = lambda f: pl.BlockSpec((1, 1, N_GATE, L), lambda b, t: (b, f(t), 0, 0))
    hshape = jax.ShapeDtypeStruct((nb, nch, BR_W, L), F32)
    return pl.pallas_call(
        _scan_kernel,
        grid=(nb, nch),
        in_specs=[rspec(BR_W, fwd), rspec(BR_W, bwd), tspec(fwd), tspec(bwd), tspec(fwd), tspec(bwd),
                  rspec(128, fwd), rspec(128, bwd), gspec(fwd), gspec(bwd)],
        out_specs=[tspec(fwd), tspec(bwd)],
        out_shape=[hshape, hshape],
        scratch_shapes=[pltpu.VMEM((2 * ML_HEADS, 2 * ML_DH, ML_DH), F32),
                        pltpu.VMEM((2 * ML_HEADS, 1, 128), F32)],
        compiler_params=_cparams("parallel", "arbitrary"),
        name="mlstm_scan",
    )(kc, kc, qt, qt, vt, vt, gl, gl, glt, glt)


NA_RB = 4
NA_BLK = NA_RB * GRID_W
NA_SLAB = 3 * NA_RB


def _na_bias_table(rpb, rows):
    c = np.arange(GRID_W)
    kcol = np.arange(GRID_W)
    cs = np.clip(c - NA_KW // 2, 0, GRID_W - NA_KW)
    ok = (kcol[None, :] >= cs[:, None]) & (kcol[None, :] < cs[:, None] + NA_KW)
    pad = GRID_W - NA_KW
    rp = jnp.pad(rpb.astype(F32), ((0, 0), (0, 0), (pad, pad)))
    toep = jnp.stack([rp[:, :, pad + NA_KW - 1 - ci:pad + NA_KW - 1 - ci + GRID_W] for ci in range(GRID_W)],
                     axis=2)
    toep = jnp.where(jnp.asarray(ok)[None, None], toep, NEG)
    masked = jnp.full((NA_HEADS, GRID_W, GRID_W), NEG, F32)
    variants = []
    for r0, s0 in ((0, 0), (2 * NA_RB, NA_RB), (rows - NA_RB, rows - NA_SLAB)):
        qrows = []
        for rr in range(NA_RB):
            r = r0 + rr
            rs = min(max(r - NA_KH // 2, 0), rows - NA_KH)
            blocks = [toep[:, s0 + i - r + NA_KH - 1] if rs <= s0 + i < rs + NA_KH else masked
                      for i in range(NA_SLAB)]
            qrows.append(jnp.concatenate(blocks, axis=-1))
        variants.append(jnp.concatenate(qrows, axis=1))
    return jnp.stack(variants, axis=0)


def _na_kernel(q_ref, k0_ref, k1_ref, k2_ref, v0_ref, v1_ref, v2_ref, kx_ref, vx_ref, z_ref, bias_ref,
               o_ref, ks, vs):
    nb = NA_BLK
    ks[0:nb] = k0_ref[0]
    ks[nb:2 * nb] = k1_ref[0]
    ks[2 * nb:3 * nb] = k2_ref[0]
    vs[0:nb] = v0_ref[0]
    vs[nb:2 * nb] = v1_ref[0]
    vs[2 * nb:3 * nb] = v2_ref[0]
    first = lax.broadcasted_iota(jnp.int32, (1, 128), 1) < NA_DH
    nt_dims = (((1,), (1,)), ((), ()))
    outs = []
    for pr in range(NA_HEADS // 2):
        cols = slice(pr * 128, (pr + 1) * 128)
        qp = q_ref[0, :, cols].astype(F32) * (NA_DH ** -0.5)
        qq = jnp.concatenate([jnp.where(first, qp, 0.0), jnp.where(first, 0.0, qp)], axis=0).astype(BF16)
        bias = jnp.concatenate([bias_ref[0, 2 * pr], bias_ref[0, 2 * pr + 1]], axis=0)
        s_loc = lax.dot_general(qq, ks[:, cols], nt_dims, preferred_element_type=F32) + bias
        s_ctx = lax.dot_general(qq, kx_ref[0, :, cols], nt_dims, preferred_element_type=F32)
        m = jnp.maximum(jnp.max(s_loc, axis=-1, keepdims=True), jnp.max(s_ctx, axis=-1, keepdims=True))
        p_loc = jnp.exp(s_loc - m)
        p_ctx = jnp.exp(s_ctx - m)
        l = jnp.sum(p_loc, axis=-1, keepdims=True) + jnp.sum(p_ctx, axis=-1, keepdims=True)
        o = (jnp.dot(p_loc.astype(BF16), vs[:, cols], preferred_element_type=F32)
             + jnp.dot(p_ctx.astype(BF16), vx_ref[0, :, cols], preferred_element_type=F32)) / l
        outs.append(jnp.where(first, o[:nb], o[nb:]))
    o_all = jnp.concatenate(outs, axis=-1)
    o_ref[0] = (o_all * _silu(z_ref[0].astype(F32))).astype(o_ref.dtype)


def _neighbourhood_attention(p, bias_tab, n_lat, n_ctx):
    nb, nt, _ = p.shape
    nblk = n_lat // NA_BLK
    blk = NA_BLK
    cblk = n_lat // n_ctx
    base = lambda i: jnp.clip(i - 1, 0, nblk - 3)
    variant = lambda i: jnp.where(i == 0, 0, jnp.where(i == nblk - 1, 2, 1))
    spec = lambda unit, off: pl.BlockSpec((1, blk, BR_W), lambda b, i: (b, base(i) + off, unit))
    own = lambda unit: pl.BlockSpec((1, blk, BR_W), lambda b, i: (b, i, unit))
    return pl.pallas_call(
        _na_kernel,
        grid=(nb, nblk),
        in_specs=[own(U_CQ),
                  spec(U_CK, 0), spec(U_CK, 1), spec(U_CK, 2),
                  spec(U_CV, 0), spec(U_CV, 1), spec(U_CV, 2),
                  pl.BlockSpec((1, n_ctx, BR_W), lambda b, i: (b, cblk, U_CK)),
                  pl.BlockSpec((1, n_ctx, BR_W), lambda b, i: (b, cblk, U_CV)),
                  own(U_CZ),
                  pl.BlockSpec((1, NA_HEADS, blk, NA_SLAB * GRID_W), lambda b, i: (variant(i), 0, 0, 0))],
        out_specs=pl.BlockSpec((1, blk, BR_W), lambda b, i: (b, i, 0)),
        out_shape=jax.ShapeDtypeStruct((nb, n_lat, BR_W), BF16),
        scratch_shapes=[pltpu.VMEM((3 * blk, BR_W), BF16), pltpu.VMEM((3 * blk, BR_W), BF16)],
        compiler_params=_cparams("parallel", "parallel"),
        name="neighbourhood_attention",
    )(p, p, p, p, p, p, p, p, p, p, bias_tab)


def _ctx_kernel(lq_ref, lk_ref, g_ref, aqk_ref, avz_ref, cqk_ref, cvz_ref, ya_ref, yc_ref, *, lam_init):
    lam = _lambda(lq_ref, lk_ref, lam_init)
    first = lax.broadcasted_iota(jnp.int32, (1, 128), 1) < 64
    second = jnp.logical_not(first)
    nt_dims = (((1,), (1,)), ((), ()))

    def softmax(s):
        e = jnp.exp(s - jnp.max(s, axis=-1, keepdims=True))
        return e / jnp.sum(e, axis=-1, keepdims=True)

    def masked_scores(q, k, msk):
        return lax.dot_general(jnp.where(msk, q, 0.0).astype(BF16), k, nt_dims, preferred_element_type=F32)

    outs = []
    for h in range(DA_HEADS):
        cols = slice(h * 128, (h + 1) * 128)
        q = aqk_ref[0, :, cols].astype(F32) * (DA_DH ** -0.5)
        k = aqk_ref[0, :, BR_W + h * 128:BR_W + (h + 1) * 128]
        v = avz_ref[0, :, cols]
        z = avz_ref[0, :, BR_W + h * 128:BR_W + (h + 1) * 128].astype(F32)
        a = softmax(masked_scores(q, k, first)) - lam * softmax(masked_scores(q, k, second))
        o = jnp.dot(a.astype(BF16), v, preferred_element_type=F32)
        ms = jnp.mean(o * o, axis=-1, keepdims=True)
        y = o * lax.rsqrt(ms + EPS) * g_ref[...] * (1.0 - lam_init)
        outs.append(y * _silu(z))
    ya_ref[0] = jnp.concatenate(outs, axis=-1).astype(ya_ref.dtype)

    outs = []
    for pr in range(NA_HEADS // 2):
        cols = slice(pr * 128, (pr + 1) * 128)
        q = cqk_ref[0, :, cols].astype(F32) * (NA_DH ** -0.5)
        k = cqk_ref[0, :, BR_W + pr * 128:BR_W + (pr + 1) * 128]
        v = cvz_ref[0, :, cols]
        pair = [jnp.dot(softmax(masked_scores(q, k, msk)).astype(BF16), v, preferred_element_type=F32)
                for msk in (first, second)]
        outs.append(jnp.where(first, pair[0], pair[1]))
    z = cvz_ref[0, :, BR_W:].astype(F32)
    yc_ref[0] = (jnp.concatenate(outs, axis=-1) * _silu(z)).astype(yc_ref.dtype)


def _ctx_attention(p, lam_q, lam_k, da_g, lam_init, n_lat, n_ctx):
    nb = p.shape[0]
    cblk = n_lat // n_ctx
    pspec = lambda unit: pl.BlockSpec((1, n_ctx, 2 * BR_W), lambda b: (b, cblk, unit // 2))
    yspec = pl.BlockSpec((1, n_ctx, BR_W), lambda b: (b, 0, 0))
    small = lambda shape: pl.BlockSpec(shape, lambda b: (0, 0))
    yshape = jax.ShapeDtypeStruct((nb, n_ctx, BR_W), BF16)
    return pl.pallas_call(
        functools.partial(_ctx_kernel, lam_init=lam_init),
        grid=(nb,),
        in_specs=[small((2, DA_DH)), small((2, DA_DH)), small((1, DA_DV)),
                  pspec(U_AQ), pspec(U_AV), pspec(U_CQ), pspec(U_CV)],
        out_specs=[yspec, yspec],
        out_shape=[yshape, yshape],
        compiler_params=_cparams("parallel"),
        name="ctx_attention",
    )(lam_q, lam_k, da_g.reshape(1, DA_DV), p, p, p, p)


def _merge_kernel(x_ref, ya_ref, yc_ref, hf_ref, hb_ref, bo_ref, bz_ref, gm0_ref, gm1_ref, gm2_ref,
                  mod_ref, mlg_ref, fg_ref, wbr_ref, wout_ref, o_ref, *, n_batch, ctx_rows, final_norm):
    b = pl.program_id(0)
    hsum = jnp.concatenate([(hf_ref[0, c] + hb_ref[0, c]).T for c in range(hf_ref.shape[1])], axis=0)
    parts = []
    for h in range(ML_HEADS):
        hh = hsum[:, h * ML_DH:(h + 1) * ML_DH]
        ms = jnp.mean(hh * hh, axis=-1, keepdims=True)
        parts.append(hh * lax.rsqrt(ms + EPS) * mlg_ref[...])
    hn = jnp.concatenate(parts, axis=-1)
    yb = (jax.nn.sigmoid(bo_ref[0].astype(F32)) * hn * _silu(bz_ref[0].astype(F32))).astype(BF16)

    def gated(gm_ref, y, n):
        return jax.nn.sigmoid(gm_ref[0].astype(F32)) * jnp.dot(y, wbr_ref[n], preferred_element_type=F32)

    merged = gated(gm0_ref, ya_ref[0], 0) + gated(gm1_ref, yb, 1) + gated(gm2_ref, yc_ref[0], 2)
    upd = jnp.dot(merged.astype(BF16), wout_ref[...], preferred_element_type=F32)

    gate = mod_ref[n_batch:n_batch + 1, 2 * D_MODEL:] if ctx_rows else mod_ref[pl.ds(b, 1), 2 * D_MODEL:]
    xo = x_ref[0] + gate * upd
    if final_norm:
        ms = jnp.mean(xo * xo, axis=-1, keepdims=True)
        xo = xo * lax.rsqrt(ms + EPS) * fg_ref[...]
    o_ref[0] = xo


def _merge(xa, p, ya, yc, hf, hb, mod, ml_g, final_g, wbr, wout, n_rows, tm, row_off, ctx_rows, final_norm):
    nb, _, d = xa.shape
    own = lambda width: pl.BlockSpec((1, tm, width), lambda b, i: (b, i, 0))
    row = lambda width, unit: pl.BlockSpec((1, tm, width), lambda b, i: (b, i + row_off, unit))
    hspec = pl.BlockSpec((1, tm // ML_CHUNK, BR_W, ML_CHUNK), lambda b, i: (b, i + row_off, 0, 0))
    const = lambda shape: pl.BlockSpec(shape, lambda b, i: (0,) * len(shape))
    return pl.pallas_call(
        functools.partial(_merge_kernel, n_batch=nb, ctx_rows=ctx_rows, final_norm=final_norm),
        grid=(nb, n_rows // tm),
        in_specs=[row(d, 0), own(BR_W), own(BR_W), hspec, hspec,
                  row(BR_W, U_BO), row(BR_W, U_BZ),
                  row(d, U_GM // 2), row(d, U_GM // 2 + 1), row(d, U_GM // 2 + 2),
                  const((8, 3 * d)), const((1, ML_DH)), const((1, d)),
                  const((3, BR_W, d)), const((d, d))],
        out_specs=own(d),
        out_shape=jax.ShapeDtypeStruct((nb, n_rows, d), F32),
        compiler_params=_cparams("parallel", "parallel"),
        name="merge",
    )(xa, ya, yc, hf, hb, p, p, p, p, p, mod, ml_g.reshape(1, ML_DH), final_g.reshape(1, d), wbr, wout)


def _rope_tables(n_lat, n_tok):
    t = jnp.arange(n_lat, dtype=jnp.int32)
    row = (t // GRID_W).astype(F32)
    col = (t % GRID_W).astype(F32)
    inv = ROPE_BASE ** (-jnp.arange(0, DA_DH // 2, 2, dtype=F32) / (DA_DH // 2))
    ang = jnp.concatenate([row[:, None] * inv, col[:, None] * inv], axis=-1)
    cos = jnp.repeat(jnp.cos(ang), 2, axis=-1)
    sin = jnp.repeat(jnp.sin(ang), 2, axis=-1) * jnp.tile(jnp.asarray([-1.0, 1.0], F32), DA_DH // 2)
    cos = jnp.tile(cos, (1, 2))
    sin = jnp.tile(sin, (1, 2))
    cos = jnp.concatenate([cos, jnp.ones((n_tok - n_lat, 128), F32)], axis=0)
    sin = jnp.concatenate([sin, jnp.zeros((n_tok - n_lat, 128), F32)], axis=0)
    return cos, sin


def _pad_w_in(w):
    d = w.shape[0]
    split = 9 * BR_W + N_GATE
    return jnp.concatenate([w[:, :split], jnp.zeros((d, GATE_PAD - N_GATE), w.dtype), w[:, split:]], axis=1).astype(BF16)


def kernel(x, c, ctx, c_ctx, w_mod, b_mod, norm_g, w_in, da_lam_q, da_lam_k, da_norm_g, ml_conv_w, ml_conv_b,
           ml_gate_b, ml_norm_g, na_rpb, w_br, w_out, final_g):
    nb, n_lat, d = x.shape
    n_ctx = ctx.shape[1]
    nt = n_lat + n_ctx
    depth = w_mod.shape[0]
    assert d == D_MODEL and nt % ROW_TILE == 0 and n_lat % n_ctx == 0 and n_lat % NA_BLK == 0 and n_lat >= 4 * NA_BLK and nb + 1 <= 8
    tq = 512

    cs = jnp.zeros((8, d), F32).at[:nb].set(c).at[nb].set(c_ctx)
    mods = _modulation(cs, w_mod, b_mod)
    cos_t, sin_t = _rope_tables(n_lat, nt)
    xa = jnp.concatenate([x, ctx], axis=1)

    for l in range(depth):
        last = l == depth - 1
        lam_init = 0.8 - 0.6 * math.exp(-0.3 * l)
        p, gates = _projection(xa, mods[l], norm_g[l], _pad_w_in(w_in[l]), n_lat)

        qt, kr, vt = _attn_prep(p, cos_t, sin_t)
        ya = _diff_attention(p, qt, kr, vt, da_lam_q[l], da_lam_k[l], da_norm_g[l], lam_init, n_lat, tq)

        kc, qtc, vtc, gl = _mlstm_prep(p, gates, ml_conv_w[l], ml_conv_b[l], ml_gate_b[l], n_lat)
        glt = jnp.transpose(gl[:, :, :N_GATE].reshape(nb, nt // ML_CHUNK, ML_CHUNK, N_GATE), (0, 1, 3, 2))
        hf, hb = _mlstm_scan(kc, qtc, vtc, gl, glt, n_lat)

        yc = _neighbourhood_attention(p, _na_bias_table(na_rpb[l], n_lat // GRID_W), n_lat, n_ctx)

        wbr = w_br[l].astype(BF16)
        wout = w_out[l].astype(BF16)
        merge = functools.partial(_merge, xa, p, hf=hf, hb=hb, mod=mods[l], ml_g=ml_norm_g[l], final_g=final_g,
                                  wbr=wbr, wout=wout)
        x_lat = merge(ya=ya, yc=yc, n_rows=n_lat, tm=512, row_off=0, ctx_rows=False, final_norm=last)
        if last:
            return x_lat
        ya_c, yc_c = _ctx_attention(p, da_lam_q[l], da_lam_k[l], da_norm_g[l], lam_init, n_lat, n_ctx)
        x_ctx = merge(ya=ya_c, yc=yc_c, n_rows=n_ctx, tm=n_ctx, row_off=n_lat // n_ctx, ctx_rows=True,
                      final_norm=False)
        xa = jnp.concatenate([x_lat, x_ctx], axis=1)
```

```python
import functools
import math

import numpy as np
import jax
import jax.numpy as jnp
from jax import lax
from jax.experimental import pallas as pl
from jax.experimental.pallas import tpu as pltpu

F32 = jnp.float32
BF16 = jnp.bfloat16

D_MODEL = 1024
BR_W = 512
GRID_W = 64
EPS = 1e-6
ROPE_BASE = 10000.0
DA_HEADS, DA_DH, DA_DV = 4, 64, 128
ML_HEADS, ML_DH, ML_CHUNK = 4, 128, 64
NA_HEADS, NA_DH, NA_KH, NA_KW = 8, 64, 8, 16
N_GATE = 4 * ML_HEADS
GATE_PAD = 512
P_W = 9 * BR_W + GATE_PAD + 4 * BR_W + 3 * D_MODEL
U_AQ, U_AK, U_AV, U_AZ, U_BQ, U_BK, U_BV, U_BO, U_BZ, U_BG, U_CQ, U_CK, U_CV, U_CZ, U_GM = range(15)
NEG = -1e30
LOG2E = math.log2(math.e)
VMEM_LIMIT = 56 * 1024 * 1024
ROW_TILE = 768


def _cparams(*sem):
    return pltpu.CompilerParams(dimension_semantics=sem, vmem_limit_bytes=VMEM_LIMIT)


def _silu(x):
    return x * jax.nn.sigmoid(x)


def _mod_kernel(c_ref, w_ref, b_ref, o_ref):
    s = _silu(c_ref[...])
    o_ref[0] = jnp.dot(s.astype(BF16), w_ref[0].astype(BF16), preferred_element_type=F32) + b_ref[0]


def _modulation(cs, w_mod, b_mod):
    depth, d, d3 = w_mod.shape
    tn = 1024
    return pl.pallas_call(
        _mod_kernel,
        grid=(depth, d3 // tn),
        in_specs=[pl.BlockSpec((8, d), lambda l, j: (0, 0)),
                  pl.BlockSpec((1, d, tn), lambda l, j: (l, 0, j)),
                  pl.BlockSpec((1, 1, tn), lambda l, j: (l, 0, j))],
        out_specs=pl.BlockSpec((1, 8, tn), lambda l, j: (l, 0, j)),
        out_shape=jax.ShapeDtypeStruct((depth, 8, d3), F32),
        compiler_params=_cparams("parallel", "parallel"),
        name="modulation",
    )(cs, w_mod, b_mod.reshape(depth, 1, d3))


PROJ_TN = 1024
GATE_TILE = (U_BG * BR_W) // PROJ_TN
GATE_OFF = (U_BG * BR_W) % PROJ_TN


def _proj_kernel(x_ref, mod_ref, g_ref, w_ref, o_ref, gate_ref, h_scr, *, n_lat, n_batch, tm):
    b = pl.program_id(0)
    i = pl.program_id(1)
    j = pl.program_id(2)

    @pl.when(j == 0)
    def _():
        x = x_ref[0]
        ms = jnp.mean(x * x, axis=-1, keepdims=True)
        y = x * lax.rsqrt(ms + EPS) * g_ref[...]
        row = i * tm + lax.broadcasted_iota(jnp.int32, (tm, 1), 0)
        is_ctx = row >= n_lat
        mb = mod_ref[pl.ds(b, 1), :]
        mc = mod_ref[n_batch:n_batch + 1, :]
        shift = jnp.where(is_ctx, mc[:, :D_MODEL], mb[:, :D_MODEL])
        scale = jnp.where(is_ctx, mc[:, D_MODEL:2 * D_MODEL], mb[:, D_MODEL:2 * D_MODEL])
        h_scr[...] = (y * (1.0 + scale) + shift).astype(BF16)

    acc = jnp.dot(h_scr[...], w_ref[...], preferred_element_type=F32)
    o_ref[0] = acc.astype(BF16)

    @pl.when(j == GATE_TILE)
    def _():
        gate_ref[0] = acc[:, GATE_OFF:GATE_OFF + 128]


def _projection(xa, mod, norm_g, w_pad, n_lat):
    nb, nt, d = xa.shape
    tm, tn = ROW_TILE, PROJ_TN
    return pl.pallas_call(
        functools.partial(_proj_kernel, n_lat=n_lat, n_batch=nb, tm=tm),
        grid=(nb, nt // tm, P_W // tn),
        in_specs=[pl.BlockSpec((1, tm, d), lambda b, i, j: (b, i, 0)),
                  pl.BlockSpec((8, 3 * d), lambda b, i, j: (0, 0)),
                  pl.BlockSpec((1, d), lambda b, i, j: (0, 0)),
                  pl.BlockSpec((d, tn), lambda b, i, j: (0, j))],
        out_specs=[pl.BlockSpec((1, tm, tn), lambda b, i, j: (b, i, j)),
                   pl.BlockSpec((1, tm, 128), lambda b, i, j: (b, i, 0))],
        out_shape=[jax.ShapeDtypeStruct((nb, nt, P_W), BF16),
                   jax.ShapeDtypeStruct((nb, nt, 128), F32)],
        scratch_shapes=[pltpu.VMEM((tm, d), BF16)],
        compiler_params=_cparams("parallel", "parallel", "arbitrary"),
        name="norm_proj",
    )(xa, mod, norm_g.reshape(1, d), w_pad)


def _aprep_kernel(qk_ref, v_ref, cos_ref, sin_ref, qt_ref, k_ref, vt_ref):
    cosv = cos_ref[...]
    sinv = sin_ref[...]
    lane = lax.broadcasted_iota(jnp.int32, (1, 128), 1)
    even = (lane % 2) == 0
    first = lane < DA_DH

    def rope(x):
        swapped = jnp.where(even, pltpu.roll(x, 127, 1), pltpu.roll(x, 1, 1))
        return x * cosv + swapped * sinv

    for h in range(DA_HEADS):
        q = rope(qk_ref[0, :, h * 128:(h + 1) * 128].astype(F32)) * (DA_DH ** -0.5 * LOG2E)
        k = rope(qk_ref[0, :, BR_W + h * 128:BR_W + (h + 1) * 128].astype(F32))
        k_ref[0, :, h * 128:(h + 1) * 128] = k.astype(BF16)
        qt_ref[0, 2 * h] = jnp.where(first, q, 0.0).T.astype(BF16)
        qt_ref[0, 2 * h + 1] = jnp.where(first, 0.0, q).T.astype(BF16)
        vt_ref[0, 0, h * 128:(h + 1) * 128, :] = v_ref[0, :, h * 128:(h + 1) * 128].astype(F32).T.astype(BF16)


def _attn_prep(p, cos_t, sin_t):
    nb, nt, _ = p.shape
    tm = ROW_TILE
    nti = nt // tm
    return pl.pallas_call(
        _aprep_kernel,
        grid=(nb, nti),
        in_specs=[pl.BlockSpec((1, tm, 2 * BR_W), lambda b, i: (b, i, 0)),
                  pl.BlockSpec((1, tm, BR_W), lambda b, i: (b, i, U_AV)),
                  pl.BlockSpec((tm, 128), lambda b, i: (i, 0)),
                  pl.BlockSpec((tm, 128), lambda b, i: (i, 0))],
        out_specs=[pl.BlockSpec((1, 2 * DA_HEADS, 128, tm), lambda b, i: (b, 0, 0, i)),
                   pl.BlockSpec((1, tm, BR_W), lambda b, i: (b, i, 0)),
                   pl.BlockSpec((1, 1, BR_W, tm), lambda b, i: (b, i, 0, 0))],
        out_shape=[jax.ShapeDtypeStruct((nb, 2 * DA_HEADS, 128, nt), BF16),
                   jax.ShapeDtypeStruct((nb, nt, BR_W), BF16),
                   jax.ShapeDtypeStruct((nb, nti, BR_W, tm), BF16)],
        compiler_params=_cparams("parallel", "parallel"),
        name="attn_prep",
    )(p, p, cos_t, sin_t)


def _lambda(lq_ref, lk_ref, lam_init):
    e = jnp.exp(jnp.sum(lq_ref[...] * lk_ref[...], axis=-1, keepdims=True))
    return e[0:1] - e[1:2] + lam_init


SUM_ROWS = 16


def _attn_kernel(lq_ref, lk_ref, g_ref, q1_ref, q2_ref, k_ref, vt_ref, z_ref, o_ref, acc1, acc2,
                 *, nkv, tk, lam_init):
    q1 = q1_ref[0, 0]
    q2 = q2_ref[0, 0]
    tq = q1.shape[1]
    acc1[...] = jnp.zeros_like(acc1)
    acc2[...] = jnp.zeros_like(acc2)
    ones = jnp.ones((SUM_ROWS, tk), BF16)

    def scores(q, j):
        s = jnp.dot(k_ref[0, j * tk:(j + 1) * tk, :], q, preferred_element_type=F32)
        return s, jnp.max(s, axis=0, keepdims=True)

    def accumulate(s_mx, acc, j, m):
        s, mx = s_mx
        vv = jnp.concatenate([vt_ref[0, j], ones], axis=0)
        mn = jnp.maximum(m, mx)
        p = jnp.exp2(s - mn).astype(BF16)
        acc[...] = acc[...] * jnp.exp2(m - mn) + jnp.dot(vv, p, preferred_element_type=F32)
        return mn

    m1 = m2 = jnp.full((1, tq), NEG, F32)
    cur1, cur2 = scores(q1, 0), scores(q2, 0)
    for j in range(nkv):
        if j + 1 < nkv:
            nxt1 = scores(q1, j + 1)
        m1 = accumulate(cur1, acc1, j, m1)
        if j + 1 < nkv:
            nxt2 = scores(q2, j + 1)
        m2 = accumulate(cur2, acc2, j, m2)
        cur1, cur2 = nxt1, nxt2

    lam = _lambda(lq_ref, lk_ref, lam_init)
    l1 = acc1[DA_DV:DA_DV + 1, :]
    l2 = acc2[DA_DV:DA_DV + 1, :]
    o = acc1[0:DA_DV, :] / l1 - lam * (acc2[0:DA_DV, :] / l2)
    ms = jnp.mean(o * o, axis=0, keepdims=True)
    y = (o * lax.rsqrt(ms + EPS)).T * g_ref[...] * (1.0 - lam_init)
    o_ref[0] = (y * _silu(z_ref[0].astype(F32))).astype(o_ref.dtype)


def _diff_attention(p, qt, kr, vt, lam_q, lam_k, da_g, lam_init, n_lat, tq):
    nb, nt, _ = p.shape
    nkv, tk = vt.shape[1], vt.shape[3]
    return pl.pallas_call(
        functools.partial(_attn_kernel, nkv=nkv, tk=tk, lam_init=lam_init),
        grid=(nb, DA_HEADS, n_lat // tq),
        in_specs=[pl.BlockSpec((2, DA_DH), lambda b, h, i: (0, 0)),
                  pl.BlockSpec((2, DA_DH), lambda b, h, i: (0, 0)),
                  pl.BlockSpec((1, DA_DV), lambda b, h, i: (0, 0)),
                  pl.BlockSpec((1, 1, 128, tq), lambda b, h, i: (b, 2 * h, 0, i)),
                  pl.BlockSpec((1, 1, 128, tq), lambda b, h, i: (b, 2 * h + 1, 0, i)),
                  pl.BlockSpec((1, nt, 128), lambda b, h, i: (b, 0, h)),
                  pl.BlockSpec((1, nkv, 128, tk), lambda b, h, i: (b, 0, h, 0)),
                  pl.BlockSpec((1, tq, 128), lambda b, h, i: (b, i, U_AZ * 4 + h))],
        out_specs=pl.BlockSpec((1, tq, 128), lambda b, h, i: (b, i, h)),
        out_shape=jax.ShapeDtypeStruct((nb, n_lat, BR_W), BF16),
        scratch_shapes=[pltpu.VMEM((DA_DV + SUM_ROWS, tq), F32), pltpu.VMEM((DA_DV + SUM_ROWS, tq), F32)],
        compiler_params=_cparams("parallel", "parallel", "parallel"),
        name="diff_attention",
    )(lam_q, lam_k, da_g.reshape(1, DA_DV), qt, qt, kr, vt, p)


HALO = 16


def _log_sigmoid(x):
    return jnp.minimum(x, 0.0) - jnp.log1p(jnp.exp(-jnp.abs(x)))


def _bprep_kernel(x_ref, prev_ref, next_ref, v_ref, g_ref, w_ref, cb_ref, gb_ref, k_ref, qt_ref, vt_ref, gl_ref,
                  *, n_lat, n_tok, tm):
    i = pl.program_id(1)
    x = x_ref[0].astype(F32)
    row = lax.broadcasted_iota(jnp.int32, (tm, 1), 0)
    pos = i * tm + row
    xp = jnp.where(row == 0, prev_ref[0, HALO - 1:HALO, :].astype(F32), pltpu.roll(x, 1, 0))
    xp = jnp.where((pos == 0) | (pos == n_lat), 0.0, xp)
    xn = jnp.where(row == tm - 1, next_ref[0, 0:1, :].astype(F32), pltpu.roll(x, tm - 1, 0))
    xn = jnp.where((pos == n_lat - 1) | (pos == n_tok - 1), 0.0, xn)
    y = xp * w_ref[0:1, :] + x * w_ref[1:2, :] + xn * w_ref[2:3, :] + cb_ref[...]
    y = _silu(y)
    k_ref[0] = (y[:, BR_W:] * (ML_DH ** -0.5)).astype(BF16)
    for c in range(tm // ML_CHUNK):
        rows = slice(c * ML_CHUNK, (c + 1) * ML_CHUNK)
        qt_ref[0, c] = y[rows, :BR_W].T.astype(BF16)
        vt_ref[0, c] = v_ref[0, rows, :].astype(F32).T.astype(BF16)
    g = g_ref[0] + gb_ref[...]
    gl_lane = lax.broadcasted_iota(jnp.int32, (1, 128), 1)
    is_f = (gl_lane % (2 * ML_HEADS)) >= ML_HEADS
    gl_ref[0] = jnp.where(is_f, _log_sigmoid(g), g)


def _mlstm_prep(p, gates, conv_w, conv_b, gate_b, n_lat):
    nb, nt, _ = p.shape
    tm = ROW_TILE
    hb = tm // HALO
    last = nt // HALO - 1
    gb = jnp.zeros((1, 128), F32).at[0, :N_GATE].set(gate_b.reshape(-1))
    cpt = tm // ML_CHUNK
    tspec = pl.BlockSpec((1, cpt, BR_W, ML_CHUNK), lambda b, i: (b, i, 0, 0))
    tshape = jax.ShapeDtypeStruct((nb, nt // ML_CHUNK, BR_W, ML_CHUNK), BF16)
    return pl.pallas_call(
        functools.partial(_bprep_kernel, n_lat=n_lat, n_tok=nt, tm=tm),
        grid=(nb, nt // tm),
        in_specs=[pl.BlockSpec((1, tm, 2 * BR_W), lambda b, i: (b, i, U_BQ // 2)),
                  pl.BlockSpec((1, HALO, 2 * BR_W), lambda b, i: (b, jnp.maximum(i * hb - 1, 0), U_BQ // 2)),
                  pl.BlockSpec((1, HALO, 2 * BR_W), lambda b, i: (b, jnp.minimum((i + 1) * hb, last), U_BQ // 2)),
                  pl.BlockSpec((1, tm, BR_W), lambda b, i: (b, i, U_BV)),
                  pl.BlockSpec((1, tm, 128), lambda b, i: (b, i, 0)),
                  pl.BlockSpec((3, 2 * BR_W), lambda b, i: (0, 0)),
                  pl.BlockSpec((1, 2 * BR_W), lambda b, i: (0, 0)),
                  pl.BlockSpec((1, 128), lambda b, i: (0, 0))],
        out_specs=[pl.BlockSpec((1, tm, BR_W), lambda b, i: (b, i, 0)), tspec, tspec,
                   pl.BlockSpec((1, tm, 128), lambda b, i: (b, i, 0))],
        out_shape=[jax.ShapeDtypeStruct((nb, nt, BR_W), BF16), tshape, tshape,
                   jax.ShapeDtypeStruct((nb, nt, 128), F32)],
        compiler_params=_cparams("parallel", "parallel"),
        name="mlstm_prep",
    )(p, p, p, p, gates, conv_w, conv_b.reshape(1, -1), gb)


def _split_dot(a, b_f32, a_is_tri):
    hi = b_f32.astype(BF16)
    lo = (b_f32 - hi.astype(F32)).astype(BF16)
    if a_is_tri:
        return jnp.dot(a, hi, preferred_element_type=F32) + jnp.dot(a, lo, preferred_element_type=F32)
    return jnp.dot(hi, a, preferred_element_type=F32) + jnp.dot(lo, a, preferred_element_type=F32)


def _scan_kernel(kf_ref, kb_ref, qtf_ref, qtb_ref, vtf_ref, vtb_ref, glf_ref, glb_ref, gtf_ref, gtb_ref,
                 hf_ref, hb_ref, c_scr, m_scr):
    t = pl.program_id(0)
    L = ML_CHUNK
    n_batch = kf_ref.shape[0]

    @pl.when(t == 0)
    def _():
        c_scr[...] = jnp.zeros_like(c_scr)
        m_scr[...] = jnp.zeros_like(m_scr)

    si = lax.broadcasted_iota(jnp.int32, (L, L), 0)
    ti = lax.broadcasted_iota(jnp.int32, (L, L), 1)
    ones_t = jnp.ones((SUM_ROWS, L), BF16)
    dirs = ((kf_ref, qtf_ref, vtf_ref, glf_ref, gtf_ref, hf_ref), (kb_ref, qtb_ref, vtb_ref, glb_ref, gtb_ref, hb_ref))
    chains = [(b, d, h) for b in range(n_batch) for d in range(2) for h in range(ML_HEADS)]
    hsl = lambda h: slice(h * ML_DH, (h + 1) * ML_DH)

    seen, gates = [], {}
    for d in range(2):
        sd = (si <= ti) if d == 0 else (si >= ti)
        tri_c = ((ti <= si) if d == 0 else (ti >= si)).astype(BF16)
        seen.append(sd)
        for b in range(n_batch):
            gl = dirs[d][3][b]
            gt = dirs[d][4][b, 0]
            gates[b, d] = (gl, _split_dot(tri_c, gl, True), _split_dot(sd.astype(BF16), gt, False))

    ks = [dirs[d][0][b, :, hsl(h)] for b, d, h in chains]
    qts = [dirs[d][1][b, 0, hsl(h), :] for b, d, h in chains]
    cmats = [c_scr[i] for i in range(len(chains))]
    raw = [jnp.dot(k, qt, preferred_element_type=F32) for k, qt in zip(ks, qts)]
    inter = [jnp.dot(c.astype(BF16), qt, preferred_element_type=F32) for c, qt in zip(cmats, qts)]

    vecs, h_loc, kv_loc = [], [], []
    for i, (b, d, h) in enumerate(chains):
        gl, bc_all, br_all = gates[b, d]
        ci = d * 2 * ML_HEADS + h
        cf = ci + ML_HEADS
        last = L - 1 if d == 0 else 0
        li_c = gl[:, ci:ci + 1]
        b_c = bc_all[:, cf:cf + 1]
        b_r = br_all[cf:cf + 1, :]
        b_tot = b_c[last:last + 1, :]
        dmat = jnp.where(seen[d], b_r + (li_c - b_c), NEG)
        rmax = jnp.max(dmat, axis=0, keepdims=True)
        s = (raw[i] * jnp.exp(dmat - rmax)).astype(BF16)
        g_c = b_tot - b_c + li_c
        g_max = jnp.max(g_c, axis=0, keepdims=True)
        wk = (ks[i].astype(F32) * jnp.exp(g_c - g_max)).astype(BF16)
        vt_aug = jnp.concatenate([dirs[d][2][b, 0, hsl(h), :], ones_t], axis=0)
        h_loc.append(jnp.dot(vt_aug, s, preferred_element_type=F32))
        kv_loc.append(jnp.dot(vt_aug, wk, preferred_element_type=F32))
        vecs.append((b_r, b_tot, rmax, g_max))

    for i, (b, d, h) in enumerate(chains):
        b_r, b_tot, rmax, g_max = vecs[i]
        m = m_scr[i][:, 0:1]
        m_new = jnp.maximum(b_tot + m, g_max)
        c_scr[i] = jnp.exp(b_tot + m - m_new) * cmats[i] + jnp.exp(g_max - m_new) * kv_loc[i]
        m_scr[i] = jnp.broadcast_to(m_new, (1, 128))
        m_inter = b_r + m
        m_t = jnp.maximum(m_inter, rmax)
        num = jnp.exp(m_inter - m_t) * inter[i] + jnp.exp(rmax - m_t) * h_loc[i]
        den = num[ML_DH:ML_DH + 1, :]
        dirs[d][5][b, 0, hsl(h), :] = num[:ML_DH, :] / jnp.maximum(jnp.abs(den), jnp.exp(-m_t))


def _mlstm_scan(kc, qt, vt, gl, glt, n_lat):
    nb, nt, _ = kc.shape
    L = ML_CHUNK
    nch = nt // L
    nlc = n_lat // L
    fwd = lambda t: (t + nlc) % nch
    bwd = lambda t: nch - 1 - t
    tspec = lambda f: pl.BlockSpec((nb, 1, BR_W, L), lambda t: (0, f(t), 0, 0))
    rspec = lambda width, f: pl.BlockSpec((nb, L, width), lambda t: (0, f(t), 0))
    gspec = lambda f: pl.BlockSpec((nb, 1, N_GATE, L), lambda t: (0, f(t), 0, 0))
    hshape = jax.ShapeDtypeStruct((nb, nch, BR_W, L), F32)
    n_chain = nb * 2 * ML_HEADS
    return pl.pallas_call(
        _scan_kernel,
        grid=(nch,),
        in_specs=[rspec(BR_W, fwd), rspec(BR_W, bwd), tspec(fwd), tspec(bwd), tspec(fwd), tspec(bwd),
                  rspec(128, fwd), rspec(128, bwd), gspec(fwd), gspec(bwd)],
        out_specs=[tspec(fwd), tspec(bwd)],
        out_shape=[hshape, hshape],
        scratch_shapes=[pltpu.VMEM((n_chain, ML_DH + SUM_ROWS, ML_DH), F32),
                        pltpu.VMEM((n_chain, 1, 128), F32)],
        compiler_params=_cparams("arbitrary"),
        name="mlstm_scan",
    )(kc, kc, qt, qt, vt, vt, gl, gl, glt, glt)


NA_RB = 4
NA_BLK = NA_RB * GRID_W
NA_SLAB = 3 * NA_RB


def _na_bias_table(rpb, rows):
    c = np.arange(GRID_W)
    kcol = np.arange(GRID_W)
    cs = np.clip(c - NA_KW // 2, 0, GRID_W - NA_KW)
    ok = (kcol[None, :] >= cs[:, None]) & (kcol[None, :] < cs[:, None] + NA_KW)
    pad = GRID_W - NA_KW
    rp = jnp.pad(rpb.astype(F32), ((0, 0), (0, 0), (pad, pad)))
    toep = jnp.stack([rp[:, :, pad + NA_KW - 1 - ci:pad + NA_KW - 1 - ci + GRID_W] for ci in range(GRID_W)],
                     axis=2)
    toep = jnp.where(jnp.asarray(ok)[None, None], toep, NEG)
    masked = jnp.full((NA_HEADS, GRID_W, GRID_W), NEG, F32)
    variants = []
    for r0, s0 in ((0, 0), (2 * NA_RB, NA_RB), (rows - NA_RB, rows - NA_SLAB)):
        qrows = []
        for rr in range(NA_RB):
            r = r0 + rr
            rs = min(max(r - NA_KH // 2, 0), rows - NA_KH)
            blocks = [toep[:, s0 + i - r + NA_KH - 1] if rs <= s0 + i < rs + NA_KH else masked
                      for i in range(NA_SLAB)]
            qrows.append(jnp.concatenate(blocks, axis=-1))
        variants.append(jnp.concatenate(qrows, axis=1))
    return jnp.stack(variants, axis=0)


def _na_kernel(q_ref, k0_ref, k1_ref, k2_ref, v0_ref, v1_ref, v2_ref, kx_ref, vx_ref, z_ref, bias_ref,
               o_ref, ks, vs):
    nb = NA_BLK
    ks[0:nb] = k0_ref[0]
    ks[nb:2 * nb] = k1_ref[0]
    ks[2 * nb:3 * nb] = k2_ref[0]
    vs[0:nb] = v0_ref[0]
    vs[nb:2 * nb] = v1_ref[0]
    vs[2 * nb:3 * nb] = v2_ref[0]
    first = lax.broadcasted_iota(jnp.int32, (1, 128), 1) < NA_DH
    nt_dims = (((1,), (1,)), ((), ()))
    outs = []
    for pr in range(NA_HEADS // 2):
        cols = slice(pr * 128, (pr + 1) * 128)
        qp = q_ref[0, :, cols].astype(F32) * (NA_DH ** -0.5)
        qq = jnp.concatenate([jnp.where(first, qp, 0.0), jnp.where(first, 0.0, qp)], axis=0).astype(BF16)
        bias = jnp.concatenate([bias_ref[0, 2 * pr], bias_ref[0, 2 * pr + 1]], axis=0)
        s_loc = lax.dot_general(qq, ks[:, cols], nt_dims, preferred_element_type=F32) + bias
        s_ctx = lax.dot_general(qq, kx_ref[0, :, cols], nt_dims, preferred_element_type=F32)
        m = jnp.maximum(jnp.max(s_loc, axis=-1, keepdims=True), jnp.max(s_ctx, axis=-1, keepdims=True))
        p_loc = jnp.exp(s_loc - m)
        p_ctx = jnp.exp(s_ctx - m)
        l = jnp.sum(p_loc, axis=-1, keepdims=True) + jnp.sum(p_ctx, axis=-1, keepdims=True)
        o = (jnp.dot(p_loc.astype(BF16), vs[:, cols], preferred_element_type=F32)
             + jnp.dot(p_ctx.astype(BF16), vx_ref[0, :, cols], preferred_element_type=F32)) / l
        outs.append(jnp.where(first, o[:nb], o[nb:]))
    o_all = jnp.concatenate(outs, axis=-1)
    o_ref[0] = (o_all * _silu(z_ref[0].astype(F32))).astype(o_ref.dtype)


def _neighbourhood_attention(p, bias_tab, n_lat, n_ctx):
    nb, nt, _ = p.shape
    nblk = n_lat // NA_BLK
    blk = NA_BLK
    cblk = n_lat // n_ctx
    base = lambda i: jnp.clip(i - 1, 0, nblk - 3)
    variant = lambda i: jnp.where(i == 0, 0, jnp.where(i == nblk - 1, 2, 1))
    spec = lambda unit, off: pl.BlockSpec((1, blk, BR_W), lambda b, i: (b, base(i) + off, unit))
    own = lambda unit: pl.BlockSpec((1, blk, BR_W), lambda b, i: (b, i, unit))
    return pl.pallas_call(
        _na_kernel,
        grid=(nb, nblk),
        in_specs=[own(U_CQ),
                  spec(U_CK, 0), spec(U_CK, 1), spec(U_CK, 2),
                  spec(U_CV, 0), spec(U_CV, 1), spec(U_CV, 2),
                  pl.BlockSpec((1, n_ctx, BR_W), lambda b, i: (b, cblk, U_CK)),
                  pl.BlockSpec((1, n_ctx, BR_W), lambda b, i: (b, cblk, U_CV)),
                  own(U_CZ),
                  pl.BlockSpec((1, NA_HEADS, blk, NA_SLAB * GRID_W), lambda b, i: (variant(i), 0, 0, 0))],
        out_specs=pl.BlockSpec((1, blk, BR_W), lambda b, i: (b, i, 0)),
        out_shape=jax.ShapeDtypeStruct((nb, n_lat, BR_W), BF16),
        scratch_shapes=[pltpu.VMEM((3 * blk, BR_W), BF16), pltpu.VMEM((3 * blk, BR_W), BF16)],
        compiler_params=_cparams("parallel", "parallel"),
        name="neighbourhood_attention",
    )(p, p, p, p, p, p, p, p, p, p, bias_tab)


def _ctx_kernel(lq_ref, lk_ref, g_ref, aqk_ref, avz_ref, cqk_ref, cvz_ref, ya_ref, yc_ref, *, lam_init):
    lam = _lambda(lq_ref, lk_ref, lam_init)
    first = lax.broadcasted_iota(jnp.int32, (1, 128), 1) < 64
    second = jnp.logical_not(first)
    nt_dims = (((1,), (1,)), ((), ()))

    def softmax(s):
        e = jnp.exp(s - jnp.max(s, axis=-1, keepdims=True))
        return e / jnp.sum(e, axis=-1, keepdims=True)

    def masked_scores(q, k, msk):
        return lax.dot_general(jnp.where(msk, q, 0.0).astype(BF16), k, nt_dims, preferred_element_type=F32)

    outs = []
    for h in range(DA_HEADS):
        cols = slice(h * 128, (h + 1) * 128)
        q = aqk_ref[0, :, cols].astype(F32) * (DA_DH ** -0.5)
        k = aqk_ref[0, :, BR_W + h * 128:BR_W + (h + 1) * 128]
        v = avz_ref[0, :, cols]
        z = avz_ref[0, :, BR_W + h * 128:BR_W + (h + 1) * 128].astype(F32)
        a = softmax(masked_scores(q, k, first)) - lam * softmax(masked_scores(q, k, second))
        o = jnp.dot(a.astype(BF16), v, preferred_element_type=F32)
        ms = jnp.mean(o * o, axis=-1, keepdims=True)
        y = o * lax.rsqrt(ms + EPS) * g_ref[...] * (1.0 - lam_init)
        outs.append(y * _silu(z))
    ya_ref[0] = jnp.concatenate(outs, axis=-1).astype(ya_ref.dtype)

    outs = []
    for pr in range(NA_HEADS // 2):
        cols = slice(pr * 128, (pr + 1) * 128)
        q = cqk_ref[0, :, cols].astype(F32) * (NA_DH ** -0.5)
        k = cqk_ref[0, :, BR_W + pr * 128:BR_W + (pr + 1) * 128]
        v = cvz_ref[0, :, cols]
        pair = [jnp.dot(softmax(masked_scores(q, k, msk)).astype(BF16), v, preferred_element_type=F32)
                for msk in (first, second)]
        outs.append(jnp.where(first, pair[0], pair[1]))
    z = cvz_ref[0, :, BR_W:].astype(F32)
    yc_ref[0] = (jnp.concatenate(outs, axis=-1) * _silu(z)).astype(yc_ref.dtype)


def _ctx_attention(p, lam_q, lam_k, da_g, lam_init, n_lat, n_ctx):
    nb = p.shape[0]
    cblk = n_lat // n_ctx
    pspec = lambda unit: pl.BlockSpec((1, n_ctx, 2 * BR_W), lambda b: (b, cblk, unit // 2))
    yspec = pl.BlockSpec((1, n_ctx, BR_W), lambda b: (b, 0, 0))
    small = lambda shape: pl.BlockSpec(shape, lambda b: (0, 0))
    yshape = jax.ShapeDtypeStruct((nb, n_ctx, BR_W), BF16)
    return pl.pallas_call(
        functools.partial(_ctx_kernel, lam_init=lam_init),
        grid=(nb,),
        in_specs=[small((2, DA_DH)), small((2, DA_DH)), small((1, DA_DV)),
                  pspec(U_AQ), pspec(U_AV), pspec(U_CQ), pspec(U_CV)],
        out_specs=[yspec, yspec],
        out_shape=[yshape, yshape],
        compiler_params=_cparams("parallel"),
        name="ctx_attention",
    )(lam_q, lam_k, da_g.reshape(1, DA_DV), p, p, p, p)


def _merge_kernel(x_ref, ya_ref, yc_ref, hf_ref, hb_ref, bo_ref, bz_ref, gm0_ref, gm1_ref, gm2_ref,
                  mod_ref, mlg_ref, fg_ref, wbr_ref, wout_ref, o_ref, *, n_batch, ctx_rows, final_norm):
    b = pl.program_id(0)
    hsum = jnp.concatenate([(hf_ref[0, c] + hb_ref[0, c]).T for c in range(hf_ref.shape[1])], axis=0)
    parts = []
    for h in range(ML_HEADS):
        hh = hsum[:, h * ML_DH:(h + 1) * ML_DH]
        ms = jnp.mean(hh * hh, axis=-1, keepdims=True)
        parts.append(hh * lax.rsqrt(ms + EPS) * mlg_ref[...])
    hn = jnp.concatenate(parts, axis=-1)
    yb = (jax.nn.sigmoid(bo_ref[0].astype(F32)) * hn * _silu(bz_ref[0].astype(F32))).astype(BF16)

    def gated(gm_ref, y, n):
        return jax.nn.sigmoid(gm_ref[0].astype(F32)) * jnp.dot(y, wbr_ref[n], preferred_element_type=F32)

    merged = gated(gm0_ref, ya_ref[0], 0) + gated(gm1_ref, yb, 1) + gated(gm2_ref, yc_ref[0], 2)
    upd = jnp.dot(merged.astype(BF16), wout_ref[...], preferred_element_type=F32)

    gate = mod_ref[n_batch:n_batch + 1, 2 * D_MODEL:] if ctx_rows else mod_ref[pl.ds(b, 1), 2 * D_MODEL:]
    xo = x_ref[0] + gate * upd
    if final_norm:
        ms = jnp.mean(xo * xo, axis=-1, keepdims=True)
        xo = xo * lax.rsqrt(ms + EPS) * fg_ref[...]
    o_ref[0] = xo


def _merge(xa, p, ya, yc, hf, hb, mod, ml_g, final_g, wbr, wout, n_rows, tm, row_off, ctx_rows, final_norm):
    nb, _, d = xa.shape
    own = lambda width: pl.BlockSpec((1, tm, width), lambda b, i: (b, i, 0))
    row = lambda width, unit: pl.BlockSpec((1, tm, width), lambda b, i: (b, i + row_off, unit))
    hspec = pl.BlockSpec((1, tm // ML_CHUNK, BR_W, ML_CHUNK), lambda b, i: (b, i + row_off, 0, 0))
    const = lambda shape: pl.BlockSpec(shape, lambda b, i: (0,) * len(shape))
    return pl.pallas_call(
        functools.partial(_merge_kernel, n_batch=nb, ctx_rows=ctx_rows, final_norm=final_norm),
        grid=(nb, n_rows // tm),
        in_specs=[row(d, 0), own(BR_W), own(BR_W), hspec, hspec,
                  row(BR_W, U_BO), row(BR_W, U_BZ),
                  row(d, U_GM // 2), row(d, U_GM // 2 + 1), row(d, U_GM // 2 + 2),
                  const((8, 3 * d)), const((1, ML_DH)), const((1, d)),
                  const((3, BR_W, d)), const((d, d))],
        out_specs=own(d),
        out_shape=jax.ShapeDtypeStruct((nb, n_rows, d), F32),
        compiler_params=_cparams("parallel", "parallel"),
        name="merge",
    )(xa, ya, yc, hf, hb, p, p, p, p, p, mod, ml_g.reshape(1, ML_DH), final_g.reshape(1, d), wbr, wout)


def _rope_tables(n_lat, n_tok):
    t = jnp.arange(n_lat, dtype=jnp.int32)
    row = (t // GRID_W).astype(F32)
    col = (t % GRID_W).astype(F32)
    inv = ROPE_BASE ** (-jnp.arange(0, DA_DH // 2, 2, dtype=F32) / (DA_DH // 2))
    ang = jnp.concatenate([row[:, None] * inv, col[:, None] * inv], axis=-1)
    cos = jnp.repeat(jnp.cos(ang), 2, axis=-1)
    sin = jnp.repeat(jnp.sin(ang), 2, axis=-1) * jnp.tile(jnp.asarray([-1.0, 1.0], F32), DA_DH // 2)
    cos = jnp.tile(cos, (1, 2))
    sin = jnp.tile(sin, (1, 2))
    cos = jnp.concatenate([cos, jnp.ones((n_tok - n_lat, 128), F32)], axis=0)
    sin = jnp.concatenate([sin, jnp.zeros((n_tok - n_lat, 128), F32)], axis=0)
    return cos, sin


def _pad_w_in(w):
    d = w.shape[0]
    split = 9 * BR_W + N_GATE
    return jnp.concatenate([w[:, :split], jnp.zeros((d, GATE_PAD - N_GATE), w.dtype), w[:, split:]], axis=1).astype(BF16)


def kernel(x, c, ctx, c_ctx, w_mod, b_mod, norm_g, w_in, da_lam_q, da_lam_k, da_norm_g, ml_conv_w, ml_conv_b,
           ml_gate_b, ml_norm_g, na_rpb, w_br, w_out, final_g):
    nb, n_lat, d = x.shape
    n_ctx = ctx.shape[1]
    nt = n_lat + n_ctx
    depth = w_mod.shape[0]
    assert d == D_MODEL and nt % ROW_TILE == 0 and n_lat % n_ctx == 0 and n_lat % NA_BLK == 0 and n_lat >= 4 * NA_BLK and nb + 1 <= 8
    tq = 512

    cs = jnp.zeros((8, d), F32).at[:nb].set(c).at[nb].set(c_ctx)
    mods = _modulation(cs, w_mod, b_mod)
    cos_t, sin_t = _rope_tables(n_lat, nt)
    xa = jnp.concatenate([x, ctx], axis=1)

    for l in range(depth):
        last = l == depth - 1
        lam_init = 0.8 - 0.6 * math.exp(-0.3 * l)
        p, gates = _projection(xa, mods[l], norm_g[l], _pad_w_in(w_in[l]), n_lat)

        qt, kr, vt = _attn_prep(p, cos_t, sin_t)
        ya = _diff_attention(p, qt, kr, vt, da_lam_q[l], da_lam_k[l], da_norm_g[l], lam_init, n_lat, tq)

        kc, qtc, vtc, gl = _mlstm_prep(p, gates, ml_conv_w[l], ml_conv_b[l], ml_gate_b[l], n_lat)
        glt = jnp.transpose(gl[:, :, :N_GATE].reshape(nb, nt // ML_CHUNK, ML_CHUNK, N_GATE), (0, 1, 3, 2))
        hf, hb = _mlstm_scan(kc, qtc, vtc, gl, glt, n_lat)

        yc = _neighbourhood_attention(p, _na_bias_table(na_rpb[l], n_lat // GRID_W), n_lat, n_ctx)

        wbr = w_br[l].astype(BF16)
        wout = w_out[l].astype(BF16)
        merge = functools.partial(_merge, xa, p, hf=hf, hb=hb, mod=mods[l], ml_g=ml_norm_g[l], final_g=final_g,
                                  wbr=wbr, wout=wout)
        x_lat = merge(ya=ya, yc=yc, n_rows=n_lat, tm=512, row_off=0, ctx_rows=False, final_norm=last)
        if last:
            return x_lat
        ya_c, yc_c = _ctx_attention(p, da_lam_q[l], da_lam_k[l], da_norm_g[l], lam_init, n_lat, n_ctx)
        x_ctx = merge(ya=ya_c, yc=yc_c, n_rows=n_ctx, tm=n_ctx, row_off=n_lat // n_ctx, ctx_rows=True,
                      final_norm=False)
        xa = jnp.concatenate([x_lat, x_ctx], axis=1)
```

```python
import functools
import math

import numpy as np
import jax
import jax.numpy as jnp
from jax import lax
from jax.experimental import pallas as pl
from jax.experimental.pallas import tpu as pltpu

F32 = jnp.float32
BF16 = jnp.bfloat16

D_MODEL = 1024
BR_W = 512
GRID_W = 64
EPS = 1e-6
ROPE_BASE = 10000.0
DA_HEADS, DA_DH, DA_DV = 4, 64, 128
ML_HEADS, ML_DH, ML_CHUNK = 4, 128, 64
NA_HEADS, NA_DH, NA_KH, NA_KW = 8, 64, 8, 16
N_GATE = 4 * ML_HEADS
GATE_PAD = 512
P_W = 9 * BR_W + GATE_PAD + 4 * BR_W + 3 * D_MODEL
U_AQ, U_AK, U_AV, U_AZ, U_BQ, U_BK, U_BV, U_BO, U_BZ, U_BG, U_CQ, U_CK, U_CV, U_CZ, U_GM = range(15)
NEG = -1e30
LOG2E = math.log2(math.e)
VMEM_LIMIT = 56 * 1024 * 1024
ROW_TILE = 768


def _cparams(*sem):
    return pltpu.CompilerParams(dimension_semantics=sem, vmem_limit_bytes=VMEM_LIMIT)


def _silu(x):
    return x * jax.nn.sigmoid(x)


def _mod_kernel(c_ref, w_ref, b_ref, o_ref):
    s = _silu(c_ref[...])
    o_ref[0] = jnp.dot(s.astype(BF16), w_ref[0].astype(BF16), preferred_element_type=F32) + b_ref[0]


def _modulation(cs, w_mod, b_mod):
    depth, d, d3 = w_mod.shape
    tn = 1024
    return pl.pallas_call(
        _mod_kernel,
        grid=(depth, d3 // tn),
        in_specs=[pl.BlockSpec((8, d), lambda l, j: (0, 0)),
                  pl.BlockSpec((1, d, tn), lambda l, j: (l, 0, j)),
                  pl.BlockSpec((1, 1, tn), lambda l, j: (l, 0, j))],
        out_specs=pl.BlockSpec((1, 8, tn), lambda l, j: (l, 0, j)),
        out_shape=jax.ShapeDtypeStruct((depth, 8, d3), F32),
        compiler_params=_cparams("parallel", "parallel"),
        name="modulation",
    )(cs, w_mod, b_mod.reshape(depth, 1, d3))


PROJ_TN = 1024
GATE_TILE = (U_BG * BR_W) // PROJ_TN
GATE_OFF = (U_BG * BR_W) % PROJ_TN


def _proj_kernel(x_ref, mod_ref, g_ref, w_ref, o_ref, gate_ref, h_scr, *, n_lat, n_batch, tm):
    b = pl.program_id(0)
    i = pl.program_id(1)
    j = pl.program_id(2)

    @pl.when(j == 0)
    def _():
        x = x_ref[0]
        ms = jnp.mean(x * x, axis=-1, keepdims=True)
        y = x * lax.rsqrt(ms + EPS) * g_ref[...]
        row = i * tm + lax.broadcasted_iota(jnp.int32, (tm, 1), 0)
        is_ctx = row >= n_lat
        mb = mod_ref[pl.ds(b, 1), :]
        mc = mod_ref[n_batch:n_batch + 1, :]
        shift = jnp.where(is_ctx, mc[:, :D_MODEL], mb[:, :D_MODEL])
        scale = jnp.where(is_ctx, mc[:, D_MODEL:2 * D_MODEL], mb[:, D_MODEL:2 * D_MODEL])
        h_scr[...] = (y * (1.0 + scale) + shift).astype(BF16)

    acc = jnp.dot(h_scr[...], w_ref[...], preferred_element_type=F32)
    o_ref[0] = acc.astype(BF16)

    @pl.when(j == GATE_TILE)
    def _():
        gate_ref[0] = acc[:, GATE_OFF:GATE_OFF + 128]


def _projection(xa, mod, norm_g, w_pad, layer, n_lat):
    nb, nt, d = xa.shape
    tm, tn = ROW_TILE, PROJ_TN
    return pl.pallas_call(
        functools.partial(_proj_kernel, n_lat=n_lat, n_batch=nb, tm=tm),
        grid=(nb, nt // tm, P_W // tn),
        in_specs=[pl.BlockSpec((1, tm, d), lambda b, i, j: (b, i, 0)),
                  pl.BlockSpec((8, 3 * d), lambda b, i, j: (0, 0)),
                  pl.BlockSpec((1, d), lambda b, i, j: (0, 0)),
                  pl.BlockSpec((None, d, tn), lambda b, i, j: (layer, 0, j))],
        out_specs=[pl.BlockSpec((1, tm, tn), lambda b, i, j: (b, i, j)),
                   pl.BlockSpec((1, tm, 128), lambda b, i, j: (b, i, 0))],
        out_shape=[jax.ShapeDtypeStruct((nb, nt, P_W), BF16),
                   jax.ShapeDtypeStruct((nb, nt, 128), F32)],
        scratch_shapes=[pltpu.VMEM((tm, d), BF16)],
        compiler_params=_cparams("parallel", "parallel", "arbitrary"),
        name="norm_proj",
    )(xa, mod, norm_g.reshape(1, d), w_pad)


def _aprep_kernel(qk_ref, v_ref, cos_ref, sin_ref, qt_ref, k_ref, vt_ref):
    cosv = cos_ref[...]
    sinv = sin_ref[...]
    lane = lax.broadcasted_iota(jnp.int32, (1, 128), 1)
    even = (lane % 2) == 0
    first = lane < DA_DH

    def rope(x):
        swapped = jnp.where(even, pltpu.roll(x, 127, 1), pltpu.roll(x, 1, 1))
        return x * cosv + swapped * sinv

    for h in range(DA_HEADS):
        q = rope(qk_ref[0, :, h * 128:(h + 1) * 128].astype(F32)) * (DA_DH ** -0.5 * LOG2E)
        k = rope(qk_ref[0, :, BR_W + h * 128:BR_W + (h + 1) * 128].astype(F32))
        k_ref[0, :, h * 128:(h + 1) * 128] = k.astype(BF16)
        qt_ref[0, 2 * h] = jnp.where(first, q, 0.0).T.astype(BF16)
        qt_ref[0, 2 * h + 1] = jnp.where(first, 0.0, q).T.astype(BF16)
        vt_ref[0, 0, h * 128:(h + 1) * 128, :] = v_ref[0, :, h * 128:(h + 1) * 128].astype(F32).T.astype(BF16)


def _attn_prep(p, cos_t, sin_t):
    nb, nt, _ = p.shape
    tm = ROW_TILE
    nti = nt // tm
    return pl.pallas_call(
        _aprep_kernel,
        grid=(nb, nti),
        in_specs=[pl.BlockSpec((1, tm, 2 * BR_W), lambda b, i: (b, i, 0)),
                  pl.BlockSpec((1, tm, BR_W), lambda b, i: (b, i, U_AV)),
                  pl.BlockSpec((tm, 128), lambda b, i: (i, 0)),
                  pl.BlockSpec((tm, 128), lambda b, i: (i, 0))],
        out_specs=[pl.BlockSpec((1, 2 * DA_HEADS, 128, tm), lambda b, i: (b, 0, 0, i)),
                   pl.BlockSpec((1, tm, BR_W), lambda b, i: (b, i, 0)),
                   pl.BlockSpec((1, 1, BR_W, tm), lambda b, i: (b, i, 0, 0))],
        out_shape=[jax.ShapeDtypeStruct((nb, 2 * DA_HEADS, 128, nt), BF16),
                   jax.ShapeDtypeStruct((nb, nt, BR_W), BF16),
                   jax.ShapeDtypeStruct((nb, nti, BR_W, tm), BF16)],
        compiler_params=_cparams("parallel", "parallel"),
        name="attn_prep",
    )(p, p, cos_t, sin_t)


def _lambda(lq_ref, lk_ref, lam_init):
    e = jnp.exp(jnp.sum(lq_ref[...] * lk_ref[...], axis=-1, keepdims=True))
    return e[0:1] - e[1:2] + lam_init


SUM_ROWS = 16


def _attn_kernel(lq_ref, lk_ref, g_ref, q1_ref, q2_ref, k_ref, vt_ref, z_ref, o_ref, acc1, acc2,
                 *, nkv, tk, lam_init):
    q1 = q1_ref[0, 0]
    q2 = q2_ref[0, 0]
    tq = q1.shape[1]
    acc1[...] = jnp.zeros_like(acc1)
    acc2[...] = jnp.zeros_like(acc2)
    ones = jnp.ones((SUM_ROWS, tk), BF16)

    def scores(q, j):
        s = jnp.dot(k_ref[0, j * tk:(j + 1) * tk, :], q, preferred_element_type=F32)
        return s, jnp.max(s, axis=0, keepdims=True)

    def accumulate(s_mx, acc, j, m):
        s, mx = s_mx
        vv = jnp.concatenate([vt_ref[0, j], ones], axis=0)
        mn = jnp.maximum(m, mx)
        p = jnp.exp2(s - mn).astype(BF16)
        acc[...] = acc[...] * jnp.exp2(m - mn) + jnp.dot(vv, p, preferred_element_type=F32)
        return mn

    m1 = m2 = jnp.full((1, tq), NEG, F32)
    cur1, cur2 = scores(q1, 0), scores(q2, 0)
    for j in range(nkv):
        if j + 1 < nkv:
            nxt1 = scores(q1, j + 1)
        m1 = accumulate(cur1, acc1, j, m1)
        if j + 1 < nkv:
            nxt2 = scores(q2, j + 1)
        m2 = accumulate(cur2, acc2, j, m2)
        cur1, cur2 = nxt1, nxt2

    lam = _lambda(lq_ref, lk_ref, lam_init)
    l1 = acc1[DA_DV:DA_DV + 1, :]
    l2 = acc2[DA_DV:DA_DV + 1, :]
    o = acc1[0:DA_DV, :] / l1 - lam * (acc2[0:DA_DV, :] / l2)
    ms = jnp.mean(o * o, axis=0, keepdims=True)
    y = (o * lax.rsqrt(ms + EPS)).T * g_ref[...] * (1.0 - lam_init)
    o_ref[0] = (y * _silu(z_ref[0].astype(F32))).astype(o_ref.dtype)


def _diff_attention(p, qt, kr, vt, lam_q, lam_k, da_g, lam_init, n_lat, tq):
    nb, nt, _ = p.shape
    nkv, tk = vt.shape[1], vt.shape[3]
    return pl.pallas_call(
        functools.partial(_attn_kernel, nkv=nkv, tk=tk, lam_init=lam_init),
        grid=(nb, DA_HEADS, n_lat // tq),
        in_specs=[pl.BlockSpec((2, DA_DH), lambda b, h, i: (0, 0)),
                  pl.BlockSpec((2, DA_DH), lambda b, h, i: (0, 0)),
                  pl.BlockSpec((1, DA_DV), lambda b, h, i: (0, 0)),
                  pl.BlockSpec((1, 1, 128, tq), lambda b, h, i: (b, 2 * h, 0, i)),
                  pl.BlockSpec((1, 1, 128, tq), lambda b, h, i: (b, 2 * h + 1, 0, i)),
                  pl.BlockSpec((1, nt, 128), lambda b, h, i: (b, 0, h)),
                  pl.BlockSpec((1, nkv, 128, tk), lambda b, h, i: (b, 0, h, 0)),
                  pl.BlockSpec((1, tq, 128), lambda b, h, i: (b, i, U_AZ * 4 + h))],
        out_specs=pl.BlockSpec((1, tq, 128), lambda b, h, i: (b, i, h)),
        out_shape=jax.ShapeDtypeStruct((nb, n_lat, BR_W), BF16),
        scratch_shapes=[pltpu.VMEM((DA_DV + SUM_ROWS, tq), F32), pltpu.VMEM((DA_DV + SUM_ROWS, tq), F32)],
        compiler_params=_cparams("parallel", "parallel", "parallel"),
        name="diff_attention",
    )(lam_q, lam_k, da_g.reshape(1, DA_DV), qt, qt, kr, vt, p)


HALO = 16


def _log_sigmoid(x):
    return jnp.minimum(x, 0.0) - jnp.log1p(jnp.exp(-jnp.abs(x)))


def _bprep_kernel(x_ref, prev_ref, next_ref, v_ref, g_ref, w_ref, cb_ref, gb_ref, k_ref, qt_ref, vt_ref, gl_ref,
                  *, n_lat, n_tok, tm):
    i = pl.program_id(1)
    x = x_ref[0].astype(F32)
    row = lax.broadcasted_iota(jnp.int32, (tm, 1), 0)
    pos = i * tm + row
    xp = jnp.where(row == 0, prev_ref[0, HALO - 1:HALO, :].astype(F32), pltpu.roll(x, 1, 0))
    xp = jnp.where((pos == 0) | (pos == n_lat), 0.0, xp)
    xn = jnp.where(row == tm - 1, next_ref[0, 0:1, :].astype(F32), pltpu.roll(x, tm - 1, 0))
    xn = jnp.where((pos == n_lat - 1) | (pos == n_tok - 1), 0.0, xn)
    y = xp * w_ref[0:1, :] + x * w_ref[1:2, :] + xn * w_ref[2:3, :] + cb_ref[...]
    y = _silu(y)
    k_ref[0] = (y[:, BR_W:] * (ML_DH ** -0.5)).astype(BF16)
    for c in range(tm // ML_CHUNK):
        rows = slice(c * ML_CHUNK, (c + 1) * ML_CHUNK)
        qt_ref[0, c] = y[rows, :BR_W].T.astype(BF16)
        vt_ref[0, c] = v_ref[0, rows, :].astype(F32).T.astype(BF16)
    g = g_ref[0] + gb_ref[...]
    gl_lane = lax.broadcasted_iota(jnp.int32, (1, 128), 1)
    is_f = (gl_lane % (2 * ML_HEADS)) >= ML_HEADS
    gl_ref[0] = jnp.where(is_f, _log_sigmoid(g), g)


def _mlstm_prep(p, gates, conv_w, conv_b, gate_b, n_lat):
    nb, nt, _ = p.shape
    tm = ROW_TILE
    hb = tm // HALO
    last = nt // HALO - 1
    gb = jnp.zeros((1, 128), F32).at[0, :N_GATE].set(gate_b.reshape(-1))
    cpt = tm // ML_CHUNK
    tspec = pl.BlockSpec((1, cpt, BR_W, ML_CHUNK), lambda b, i: (b, i, 0, 0))
    tshape = jax.ShapeDtypeStruct((nb, nt // ML_CHUNK, BR_W, ML_CHUNK), BF16)
    return pl.pallas_call(
        functools.partial(_bprep_kernel, n_lat=n_lat, n_tok=nt, tm=tm),
        grid=(nb, nt // tm),
        in_specs=[pl.BlockSpec((1, tm, 2 * BR_W), lambda b, i: (b, i, U_BQ // 2)),
                  pl.BlockSpec((1, HALO, 2 * BR_W), lambda b, i: (b, jnp.maximum(i * hb - 1, 0), U_BQ // 2)),
                  pl.BlockSpec((1, HALO, 2 * BR_W), lambda b, i: (b, jnp.minimum((i + 1) * hb, last), U_BQ // 2)),
                  pl.BlockSpec((1, tm, BR_W), lambda b, i: (b, i, U_BV)),
                  pl.BlockSpec((1, tm, 128), lambda b, i: (b, i, 0)),
                  pl.BlockSpec((3, 2 * BR_W), lambda b, i: (0, 0)),
                  pl.BlockSpec((1, 2 * BR_W), lambda b, i: (0, 0)),
                  pl.BlockSpec((1, 128), lambda b, i: (0, 0))],
        out_specs=[pl.BlockSpec((1, tm, BR_W), lambda b, i: (b, i, 0)), tspec, tspec,
                   pl.BlockSpec((1, tm, 128), lambda b, i: (b, i, 0))],
        out_shape=[jax.ShapeDtypeStruct((nb, nt, BR_W), BF16), tshape, tshape,
                   jax.ShapeDtypeStruct((nb, nt, 128), F32)],
        compiler_params=_cparams("parallel", "parallel"),
        name="mlstm_prep",
    )(p, p, p, p, gates, conv_w, conv_b.reshape(1, -1), gb)


def _split_dot(a, b_f32, a_is_tri):
    hi = b_f32.astype(BF16)
    lo = (b_f32 - hi.astype(F32)).astype(BF16)
    if a_is_tri:
        return jnp.dot(a, hi, preferred_element_type=F32) + jnp.dot(a, lo, preferred_element_type=F32)
    return jnp.dot(hi, a, preferred_element_type=F32) + jnp.dot(lo, a, preferred_element_type=F32)


def _scan_kernel(kf_ref, kb_ref, qtf_ref, qtb_ref, vtf_ref, vtb_ref, glf_ref, glb_ref, gtf_ref, gtb_ref,
                 hf_ref, hb_ref, c_scr, m_scr):
    t = pl.program_id(0)
    L = ML_CHUNK
    n_batch = kf_ref.shape[0]

    @pl.when(t == 0)
    def _():
        c_scr[...] = jnp.zeros_like(c_scr)
        m_scr[...] = jnp.zeros_like(m_scr)

    si = lax.broadcasted_iota(jnp.int32, (L, L), 0)
    ti = lax.broadcasted_iota(jnp.int32, (L, L), 1)
    ones_t = jnp.ones((SUM_ROWS, L), BF16)
    dirs = ((kf_ref, qtf_ref, vtf_ref, glf_ref, gtf_ref, hf_ref), (kb_ref, qtb_ref, vtb_ref, glb_ref, gtb_ref, hb_ref))
    chains = [(b, d, h) for b in range(n_batch) for d in range(2) for h in range(ML_HEADS)]
    hsl = lambda h: slice(h * ML_DH, (h + 1) * ML_DH)

    seen, gates = [], {}
    for d in range(2):
        sd = (si <= ti) if d == 0 else (si >= ti)
        tri_c = ((ti <= si) if d == 0 else (ti >= si)).astype(BF16)
        seen.append(sd)
        for b in range(n_batch):
            gl = dirs[d][3][b]
            gt = dirs[d][4][b, 0]
            gates[b, d] = (gl, _split_dot(tri_c, gl, True), _split_dot(sd.astype(BF16), gt, False))

    ks = [dirs[d][0][b, :, hsl(h)] for b, d, h in chains]
    qts = [dirs[d][1][b, 0, hsl(h), :] for b, d, h in chains]
    cmats = [c_scr[i] for i in range(len(chains))]
    raw = [jnp.dot(k, qt, preferred_element_type=F32) for k, qt in zip(ks, qts)]
    inter = [jnp.dot(c.astype(BF16), qt, preferred_element_type=F32) for c, qt in zip(cmats, qts)]

    vecs, h_loc, kv_loc = [], [], []
    for i, (b, d, h) in enumerate(chains):
        gl, bc_all, br_all = gates[b, d]
        ci = d * 2 * ML_HEADS + h
        cf = ci + ML_HEADS
        last = L - 1 if d == 0 else 0
        li_c = gl[:, ci:ci + 1]
        b_c = bc_all[:, cf:cf + 1]
        b_r = br_all[cf:cf + 1, :]
        b_tot = b_c[last:last + 1, :]
        dmat = jnp.where(seen[d], b_r + (li_c - b_c), NEG)
        rmax = jnp.max(dmat, axis=0, keepdims=True)
        s = (raw[i] * jnp.exp(dmat - rmax)).astype(BF16)
        g_c = b_tot - b_c + li_c
        g_max = jnp.max(g_c, axis=0, keepdims=True)
        wk = (ks[i].astype(F32) * jnp.exp(g_c - g_max)).astype(BF16)
        vt_aug = jnp.concatenate([dirs[d][2][b, 0, hsl(h), :], ones_t], axis=0)
        h_loc.append(jnp.dot(vt_aug, s, preferred_element_type=F32))
        kv_loc.append(jnp.dot(vt_aug, wk, preferred_element_type=F32))
        vecs.append((b_r, b_tot, rmax, g_max))

    for i, (b, d, h) in enumerate(chains):
        b_r, b_tot, rmax, g_max = vecs[i]
        m = m_scr[i][:, 0:1]
        m_new = jnp.maximum(b_tot + m, g_max)
        c_scr[i] = jnp.exp(b_tot + m - m_new) * cmats[i] + jnp.exp(g_max - m_new) * kv_loc[i]
        m_scr[i] = jnp.broadcast_to(m_new, (1, 128))
        m_inter = b_r + m
        m_t = jnp.maximum(m_inter, rmax)
        num = jnp.exp(m_inter - m_t) * inter[i] + jnp.exp(rmax - m_t) * h_loc[i]
        den = num[ML_DH:ML_DH + 1, :]
        dirs[d][5][b, 0, hsl(h), :] = num[:ML_DH, :] / jnp.maximum(jnp.abs(den), jnp.exp(-m_t))


def _mlstm_scan(kc, qt, vt, gl, glt, n_lat):
    nb, nt, _ = kc.shape
    L = ML_CHUNK
    nch = nt // L
    nlc = n_lat // L
    fwd = lambda t: (t + nlc) % nch
    bwd = lambda t: nch - 1 - t
    tspec = lambda f: pl.BlockSpec((nb, 1, BR_W, L), lambda t: (0, f(t), 0, 0))
    rspec = lambda width, f: pl.BlockSpec((nb, L, width), lambda t: (0, f(t), 0))
    gspec = lambda f: pl.BlockSpec((nb, 1, N_GATE, L), lambda t: (0, f(t), 0, 0))
    hshape = jax.ShapeDtypeStruct((nb, nch, BR_W, L), F32)
    n_chain = nb * 2 * ML_HEADS
    return pl.pallas_call(
        _scan_kernel,
        grid=(nch,),
        in_specs=[rspec(BR_W, fwd), rspec(BR_W, bwd), tspec(fwd), tspec(bwd), tspec(fwd), tspec(bwd),
                  rspec(128, fwd), rspec(128, bwd), gspec(fwd), gspec(bwd)],
        out_specs=[tspec(fwd), tspec(bwd)],
        out_shape=[hshape, hshape],
        scratch_shapes=[pltpu.VMEM((n_chain, ML_DH + SUM_ROWS, ML_DH), F32),
                        pltpu.VMEM((n_chain, 1, 128), F32)],
        compiler_params=_cparams("arbitrary"),
        name="mlstm_scan",
    )(kc, kc, qt, qt, vt, vt, gl, gl, glt, glt)


NA_RB = 4
NA_BLK = NA_RB * GRID_W
NA_SLAB = 3 * NA_RB


def _toeplitz_kernel(rpb_ref, sel_ref, ok_ref, o_ref):
    x = rpb_ref[...]
    hi = x.astype(BF16)
    r1 = x - hi.astype(F32)
    mid = r1.astype(BF16)
    lo = (r1 - mid.astype(F32)).astype(BF16)
    sel = sel_ref[...]
    y = (jnp.dot(hi, sel, preferred_element_type=F32) + jnp.dot(mid, sel, preferred_element_type=F32)
         + jnp.dot(lo, sel, preferred_element_type=F32))
    o_ref[...] = jnp.where(ok_ref[...] > 0.5, y, NEG)


def _na_bias_tables(rpb_all, rows):
    depth = rpb_all.shape[0]
    nrow, ncol = 2 * NA_KH - 1, 2 * NA_KW - 1
    c = np.arange(GRID_W)
    kcol = np.arange(GRID_W)
    cs = np.clip(c - NA_KW // 2, 0, GRID_W - NA_KW)
    ok = ((kcol[None, :] >= cs[:, None]) & (kcol[None, :] < cs[:, None] + NA_KW)).reshape(1, -1)
    cidx = np.clip(kcol[None, :] - c[:, None] + NA_KW - 1, 0, ncol - 1).reshape(-1)
    sel = np.zeros((128, GRID_W * GRID_W), np.float32)
    sel[cidx, np.arange(GRID_W * GRID_W)] = 1.0
    flat = jnp.pad(rpb_all.astype(F32).reshape(depth * NA_HEADS * nrow, ncol), ((0, 0), (0, 128 - ncol)))
    toep = pl.pallas_call(
        _toeplitz_kernel,
        out_shape=jax.ShapeDtypeStruct((flat.shape[0], GRID_W * GRID_W), F32),
        compiler_params=pltpu.CompilerParams(vmem_limit_bytes=VMEM_LIMIT),
        name="na_bias_toeplitz",
    )(flat, jnp.asarray(sel, BF16), jnp.asarray(ok, F32))
    toep = toep.reshape(depth, NA_HEADS, nrow, GRID_W * GRID_W)
    pieces = []
    for r0, s0 in ((0, 0), (2 * NA_RB, NA_RB), (rows - NA_RB, rows - NA_SLAB)):
        for rr in range(NA_RB):
            r = r0 + rr
            rs = min(max(r - NA_KH // 2, 0), rows - NA_KH)
            first = rs - r + NA_KH - 1
            piece = toep[:, :, first:first + NA_KH]
            pieces.append(jnp.pad(piece, ((0, 0), (0, 0), (rs - s0, NA_SLAB - NA_KH - (rs - s0)), (0, 0)),
                                  constant_values=NEG))
    tab = jnp.stack(pieces, axis=2).reshape(depth, NA_HEADS, 3, NA_RB, NA_SLAB, GRID_W, GRID_W)
    tab = jnp.transpose(tab, (0, 2, 1, 3, 5, 4, 6))
    return tab.reshape(depth, 3, NA_HEADS, NA_BLK, NA_SLAB * GRID_W)


def _na_kernel(q_ref, k0_ref, k1_ref, k2_ref, v0_ref, v1_ref, v2_ref, kx_ref, vx_ref, z_ref, bias_ref,
               o_ref, ks, vs):
    nb = NA_BLK
    ks[0:nb] = k0_ref[0]
    ks[nb:2 * nb] = k1_ref[0]
    ks[2 * nb:3 * nb] = k2_ref[0]
    vs[0:nb] = v0_ref[0]
    vs[nb:2 * nb] = v1_ref[0]
    vs[2 * nb:3 * nb] = v2_ref[0]
    first = lax.broadcasted_iota(jnp.int32, (1, 128), 1) < NA_DH
    nt_dims = (((1,), (1,)), ((), ()))
    outs = []
    for pr in range(NA_HEADS // 2):
        cols = slice(pr * 128, (pr + 1) * 128)
        qp = q_ref[0, :, cols].astype(F32) * (NA_DH ** -0.5)
        qq = jnp.concatenate([jnp.where(first, qp, 0.0), jnp.where(first, 0.0, qp)], axis=0).astype(BF16)
        bias = jnp.concatenate([bias_ref[0, 2 * pr], bias_ref[0, 2 * pr + 1]], axis=0)
        s_loc = lax.dot_general(qq, ks[:, cols], nt_dims, preferred_element_type=F32) + bias
        s_ctx = lax.dot_general(qq, kx_ref[0, :, cols], nt_dims, preferred_element_type=F32)
        m = jnp.maximum(jnp.max(s_loc, axis=-1, keepdims=True), jnp.max(s_ctx, axis=-1, keepdims=True))
        p_loc = jnp.exp(s_loc - m)
        p_ctx = jnp.exp(s_ctx - m)
        l = jnp.sum(p_loc, axis=-1, keepdims=True) + jnp.sum(p_ctx, axis=-1, keepdims=True)
        o = (jnp.dot(p_loc.astype(BF16), vs[:, cols], preferred_element_type=F32)
             + jnp.dot(p_ctx.astype(BF16), vx_ref[0, :, cols], preferred_element_type=F32)) / l
        outs.append(jnp.where(first, o[:nb], o[nb:]))
    o_all = jnp.concatenate(outs, axis=-1)
    o_ref[0] = (o_all * _silu(z_ref[0].astype(F32))).astype(o_ref.dtype)


def _neighbourhood_attention(p, bias_tab, layer, n_lat, n_ctx):
    nb, nt, _ = p.shape
    nblk = n_lat // NA_BLK
    blk = NA_BLK
    cblk = n_lat // n_ctx
    base = lambda i: jnp.clip(i - 1, 0, nblk - 3)
    variant = lambda i: jnp.where(i == 0, 0, jnp.where(i == nblk - 1, 2, 1))
    spec = lambda unit, off: pl.BlockSpec((1, blk, BR_W), lambda b, i: (b, base(i) + off, unit))
    own = lambda unit: pl.BlockSpec((1, blk, BR_W), lambda b, i: (b, i, unit))
    return pl.pallas_call(
        _na_kernel,
        grid=(nb, nblk),
        in_specs=[own(U_CQ),
                  spec(U_CK, 0), spec(U_CK, 1), spec(U_CK, 2),
                  spec(U_CV, 0), spec(U_CV, 1), spec(U_CV, 2),
                  pl.BlockSpec((1, n_ctx, BR_W), lambda b, i: (b, cblk, U_CK)),
                  pl.BlockSpec((1, n_ctx, BR_W), lambda b, i: (b, cblk, U_CV)),
                  own(U_CZ),
                  pl.BlockSpec((None, 1, NA_HEADS, blk, NA_SLAB * GRID_W),
                               lambda b, i: (layer, variant(i), 0, 0, 0))],
        out_specs=pl.BlockSpec((1, blk, BR_W), lambda b, i: (b, i, 0)),
        out_shape=jax.ShapeDtypeStruct((nb, n_lat, BR_W), BF16),
        scratch_shapes=[pltpu.VMEM((3 * blk, BR_W), BF16), pltpu.VMEM((3 * blk, BR_W), BF16)],
        compiler_params=_cparams("parallel", "parallel"),
        name="neighbourhood_attention",
    )(p, p, p, p, p, p, p, p, p, p, bias_tab)


def _ctx_kernel(lq_ref, lk_ref, g_ref, aqk_ref, avz_ref, cqk_ref, cvz_ref, ya_ref, yc_ref, *, lam_init):
    lam = _lambda(lq_ref, lk_ref, lam_init)
    first = lax.broadcasted_iota(jnp.int32, (1, 128), 1) < 64
    second = jnp.logical_not(first)
    nt_dims = (((1,), (1,)), ((), ()))

    def softmax(s):
        e = jnp.exp(s - jnp.max(s, axis=-1, keepdims=True))
        return e / jnp.sum(e, axis=-1, keepdims=True)

    def masked_scores(q, k, msk):
        return lax.dot_general(jnp.where(msk, q, 0.0).astype(BF16), k, nt_dims, preferred_element_type=F32)

    outs = []
    for h in range(DA_HEADS):
        cols = slice(h * 128, (h + 1) * 128)
        q = aqk_ref[0, :, cols].astype(F32) * (DA_DH ** -0.5)
        k = aqk_ref[0, :, BR_W + h * 128:BR_W + (h + 1) * 128]
        v = avz_ref[0, :, cols]
        z = avz_ref[0, :, BR_W + h * 128:BR_W + (h + 1) * 128].astype(F32)
        a = softmax(masked_scores(q, k, first)) - lam * softmax(masked_scores(q, k, second))
        o = jnp.dot(a.astype(BF16), v, preferred_element_type=F32)
        ms = jnp.mean(o * o, axis=-1, keepdims=True)
        y = o * lax.rsqrt(ms + EPS) * g_ref[...] * (1.0 - lam_init)
        outs.append(y * _silu(z))
    ya_ref[0] = jnp.concatenate(outs, axis=-1).astype(ya_ref.dtype)

    outs = []
    for pr in range(NA_HEADS // 2):
        cols = slice(pr * 128, (pr + 1) * 128)
        q = cqk_ref[0, :, cols].astype(F32) * (NA_DH ** -0.5)
        k = cqk_ref[0, :, BR_W + pr * 128:BR_W + (pr + 1) * 128]
        v = cvz_ref[0, :, cols]
        pair = [jnp.dot(softmax(masked_scores(q, k, msk)).astype(BF16), v, preferred_element_type=F32)
                for msk in (first, second)]
        outs.append(jnp.where(first, pair[0], pair[1]))
    z = cvz_ref[0, :, BR_W:].astype(F32)
    yc_ref[0] = (jnp.concatenate(outs, axis=-1) * _silu(z)).astype(yc_ref.dtype)


def _ctx_attention(p, lam_q, lam_k, da_g, lam_init, n_lat, n_ctx):
    nb = p.shape[0]
    cblk = n_lat // n_ctx
    pspec = lambda unit: pl.BlockSpec((1, n_ctx, 2 * BR_W), lambda b: (b, cblk, unit // 2))
    yspec = pl.BlockSpec((1, n_ctx, BR_W), lambda b: (b, 0, 0))
    small = lambda shape: pl.BlockSpec(shape, lambda b: (0, 0))
    yshape = jax.ShapeDtypeStruct((nb, n_ctx, BR_W), BF16)
    return pl.pallas_call(
        functools.partial(_ctx_kernel, lam_init=lam_init),
        grid=(nb,),
        in_specs=[small((2, DA_DH)), small((2, DA_DH)), small((1, DA_DV)),
                  pspec(U_AQ), pspec(U_AV), pspec(U_CQ), pspec(U_CV)],
        out_specs=[yspec, yspec],
        out_shape=[yshape, yshape],
        compiler_params=_cparams("parallel"),
        name="ctx_attention",
    )(lam_q, lam_k, da_g.reshape(1, DA_DV), p, p, p, p)


def _merge_kernel(x_ref, ya_ref, yc_ref, hf_ref, hb_ref, bo_ref, bz_ref, gm0_ref, gm1_ref, gm2_ref,
                  mod_ref, mlg_ref, fg_ref, wbr_ref, wout_ref, o_ref, *, n_batch, ctx_rows, final_norm):
    b = pl.program_id(0)
    hsum = jnp.concatenate([(hf_ref[0, c] + hb_ref[0, c]).T for c in range(hf_ref.shape[1])], axis=0)
    parts = []
    for h in range(ML_HEADS):
        hh = hsum[:, h * ML_DH:(h + 1) * ML_DH]
        ms = jnp.mean(hh * hh, axis=-1, keepdims=True)
        parts.append(hh * lax.rsqrt(ms + EPS) * mlg_ref[...])
    hn = jnp.concatenate(parts, axis=-1)
    yb = (jax.nn.sigmoid(bo_ref[0].astype(F32)) * hn * _silu(bz_ref[0].astype(F32))).astype(BF16)

    def gated(gm_ref, y, n):
        return jax.nn.sigmoid(gm_ref[0].astype(F32)) * jnp.dot(y, wbr_ref[n], preferred_element_type=F32)

    merged = gated(gm0_ref, ya_ref[0], 0) + gated(gm1_ref, yb, 1) + gated(gm2_ref, yc_ref[0], 2)
    upd = jnp.dot(merged.astype(BF16), wout_ref[...], preferred_element_type=F32)

    gate = mod_ref[n_batch:n_batch + 1, 2 * D_MODEL:] if ctx_rows else mod_ref[pl.ds(b, 1), 2 * D_MODEL:]
    xo = x_ref[0] + gate * upd
    if final_norm:
        ms = jnp.mean(xo * xo, axis=-1, keepdims=True)
        xo = xo * lax.rsqrt(ms + EPS) * fg_ref[...]
    o_ref[0] = xo


def _merge(xa, p, ya, yc, hf, hb, mod, ml_g, final_g, wbr, wout, n_rows, tm, row_off, ctx_rows, final_norm):
    nb, _, d = xa.shape
    own = lambda width: pl.BlockSpec((1, tm, width), lambda b, i: (b, i, 0))
    row = lambda width, unit: pl.BlockSpec((1, tm, width), lambda b, i: (b, i + row_off, unit))
    hspec = pl.BlockSpec((1, tm // ML_CHUNK, BR_W, ML_CHUNK), lambda b, i: (b, i + row_off, 0, 0))
    const = lambda shape: pl.BlockSpec(shape, lambda b, i: (0,) * len(shape))
    return pl.pallas_call(
        functools.partial(_merge_kernel, n_batch=nb, ctx_rows=ctx_rows, final_norm=final_norm),
        grid=(nb, n_rows // tm),
        in_specs=[row(d, 0), own(BR_W), own(BR_W), hspec, hspec,
                  row(BR_W, U_BO), row(BR_W, U_BZ),
                  row(d, U_GM // 2), row(d, U_GM // 2 + 1), row(d, U_GM // 2 + 2),
                  const((8, 3 * d)), const((1, ML_DH)), const((1, d)),
                  const((3, BR_W, d)), const((d, d))],
        out_specs=own(d),
        out_shape=jax.ShapeDtypeStruct((nb, n_rows, d), F32),
        compiler_params=_cparams("parallel", "parallel"),
        name="merge",
    )(xa, ya, yc, hf, hb, p, p, p, p, p, mod, ml_g.reshape(1, ML_DH), final_g.reshape(1, d), wbr, wout)


def _rope_tables(n_lat, n_tok):
    t = jnp.arange(n_lat, dtype=jnp.int32)
    row = (t // GRID_W).astype(F32)
    col = (t % GRID_W).astype(F32)
    inv = ROPE_BASE ** (-jnp.arange(0, DA_DH // 2, 2, dtype=F32) / (DA_DH // 2))
    ang = jnp.concatenate([row[:, None] * inv, col[:, None] * inv], axis=-1)
    cos = jnp.repeat(jnp.cos(ang), 2, axis=-1)
    sin = jnp.repeat(jnp.sin(ang), 2, axis=-1) * jnp.tile(jnp.asarray([-1.0, 1.0], F32), DA_DH // 2)
    cos = jnp.tile(cos, (1, 2))
    sin = jnp.tile(sin, (1, 2))
    cos = jnp.concatenate([cos, jnp.ones((n_tok - n_lat, 128), F32)], axis=0)
    sin = jnp.concatenate([sin, jnp.zeros((n_tok - n_lat, 128), F32)], axis=0)
    return cos, sin


def _pad_w_in(w):
    depth, d, _ = w.shape
    split = 9 * BR_W + N_GATE
    zeros = jnp.zeros((depth, d, GATE_PAD - N_GATE), w.dtype)
    return jnp.concatenate([w[:, :, :split], zeros, w[:, :, split:]], axis=2).astype(BF16)


def kernel(x, c, ctx, c_ctx, w_mod, b_mod, norm_g, w_in, da_lam_q, da_lam_k, da_norm_g, ml_conv_w, ml_conv_b,
           ml_gate_b, ml_norm_g, na_rpb, w_br, w_out, final_g):
    nb, n_lat, d = x.shape
    n_ctx = ctx.shape[1]
    nt = n_lat + n_ctx
    depth = w_mod.shape[0]
    assert d == D_MODEL and nt % ROW_TILE == 0 and n_lat % n_ctx == 0 and n_lat % NA_BLK == 0 and n_lat >= 4 * NA_BLK and nb + 1 <= 8
    tq = 1024

    cs =jnp.concatenate([c, c_ctx[None], jnp.zeros((8 - nb - 1, d), F32)], axis=0)
    mods = _modulation(cs, w_mod, b_mod)
    cos_t, sin_t = _rope_tables(n_lat, nt)
    xa = jnp.concatenate([x, ctx], axis=1)
    w_in_p = _pad_w_in(w_in)
    w_br_b = w_br.astype(BF16)
    w_out_b = w_out.astype(BF16)
    bias_tabs = _na_bias_tables(na_rpb, n_lat // GRID_W)

    for l in range(depth):
        last = l == depth - 1
        lam_init = 0.8 - 0.6 * math.exp(-0.3 * l)
        p, gates = _projection(xa, mods[l], norm_g[l], w_in_p, l, n_lat)

        qt, kr, vt = _attn_prep(p, cos_t, sin_t)
        ya = _diff_attention(p, qt, kr, vt, da_lam_q[l], da_lam_k[l], da_norm_g[l], lam_init, n_lat, tq)

        kc, qtc, vtc, gl = _mlstm_prep(p, gates, ml_conv_w[l], ml_conv_b[l], ml_gate_b[l], n_lat)
        glt = jnp.transpose(gl[:, :, :N_GATE].reshape(nb, nt // ML_CHUNK, ML_CHUNK, N_GATE), (0, 1, 3, 2))
        hf, hb = _mlstm_scan(kc, qtc, vtc, gl, glt, n_lat)

        yc = _neighbourhood_attention(p, bias_tabs, l, n_lat, n_ctx)

        merge = functools.partial(_merge, xa, p, hf=hf, hb=hb, mod=mods[l], ml_g=ml_norm_g[l], final_g=final_g,
                                  wbr=w_br_b[l], wout=w_out_b[l])
        x_lat = merge(ya=ya, yc=yc, n_rows=n_lat, tm=512, row_off=0, ctx_rows=False, final_norm=last)
        if last:
            return x_lat
        ya_c, yc_c = _ctx_attention(p, da_lam_q[l], da_lam_k[l], da_norm_g[l], lam_init, n_lat, n_ctx)
        x_ctx = merge(ya=ya_c, yc=yc_c, n_rows=n_ctx, tm=n_ctx, row_off=n_lat // n_ctx, ctx_rows=True,
                      final_norm=False)
        xa = jnp.concatenate([x_lat, x_ctx], axis=1)
```

```python
import functools
import math

import numpy as np
import jax
import jax.numpy as jnp
from jax import lax
from jax.experimental import pallas as pl
from jax.experimental.pallas import tpu as pltpu

F32 = jnp.float32
BF16 = jnp.bfloat16

D_MODEL = 1024
BR_W = 512
GRID_W = 64
EPS = 1e-6
ROPE_BASE = 10000.0
DA_HEADS, DA_DH, DA_DV = 4, 64, 128
ML_HEADS, ML_DH, ML_CHUNK = 4, 128, 64
NA_HEADS, NA_DH, NA_KH, NA_KW = 8, 64, 8, 16
N_GATE = 4 * ML_HEADS
GATE_PAD = 512
P_W = 9 * BR_W + GATE_PAD + 4 * BR_W + 3 * D_MODEL
U_AQ, U_AK, U_AV, U_AZ, U_BQ, U_BK, U_BV, U_BO, U_BZ, U_BG, U_CQ, U_CK, U_CV, U_CZ, U_GM = range(15)
NEG = -1e30
LOG2E = math.log2(math.e)
VMEM_LIMIT = 56 * 1024 * 1024
ROW_TILE = 768


def _cparams(*sem):
    return pltpu.CompilerParams(dimension_semantics=sem, vmem_limit_bytes=VMEM_LIMIT)


def _silu(x):
    return x * jax.nn.sigmoid(x)


def _mod_kernel(c_ref, w_ref, b_ref, o_ref):
    s = _silu(c_ref[...])
    o_ref[0] = jnp.dot(s.astype(BF16), w_ref[0].astype(BF16), preferred_element_type=F32) + b_ref[0]


def _modulation(cs, w_mod, b_mod):
    depth, d, d3 = w_mod.shape
    tn = 1024
    return pl.pallas_call(
        _mod_kernel,
        grid=(depth, d3 // tn),
        in_specs=[pl.BlockSpec((8, d), lambda l, j: (0, 0)),
                  pl.BlockSpec((1, d, tn), lambda l, j: (l, 0, j)),
                  pl.BlockSpec((1, 1, tn), lambda l, j: (l, 0, j))],
        out_specs=pl.BlockSpec((1, 8, tn), lambda l, j: (l, 0, j)),
        out_shape=jax.ShapeDtypeStruct((depth, 8, d3), F32),
        compiler_params=_cparams("parallel", "parallel"),
        name="modulation",
    )(cs, w_mod, b_mod.reshape(depth, 1, d3))


PROJ_TN = 1024
GATE_TILE = (U_BG * BR_W) // PROJ_TN
GATE_OFF = (U_BG * BR_W) % PROJ_TN


def _proj_kernel(x_ref, mod_ref, g_ref, w_ref, o_ref, gate_ref, h_scr, *, n_lat, n_batch, tm):
    b = pl.program_id(0)
    i = pl.program_id(1)
    j = pl.program_id(2)

    @pl.when(j == 0)
    def _():
        x = x_ref[0]
        ms = jnp.mean(x * x, axis=-1, keepdims=True)
        y = x * lax.rsqrt(ms + EPS) * g_ref[...]
        row = i * tm + lax.broadcasted_iota(jnp.int32, (tm, 1), 0)
        is_ctx = row >= n_lat
        mb = mod_ref[pl.ds(b, 1), :]
        mc = mod_ref[n_batch:n_batch + 1, :]
        shift = jnp.where(is_ctx, mc[:, :D_MODEL], mb[:, :D_MODEL])
        scale = jnp.where(is_ctx, mc[:, D_MODEL:2 * D_MODEL], mb[:, D_MODEL:2 * D_MODEL])
        h_scr[...] = (y * (1.0 + scale) + shift).astype(BF16)

    acc = jnp.dot(h_scr[...], w_ref[...], preferred_element_type=F32)
    o_ref[0] = acc.astype(BF16)

    @pl.when(j == GATE_TILE)
    def _():
        gate_ref[0] = acc[:, GATE_OFF:GATE_OFF + 128]


def _projection(xa, mod, norm_g, w_pad, layer, n_lat):
    nb, nt, d = xa.shape
    tm, tn = ROW_TILE, PROJ_TN
    return pl.pallas_call(
        functools.partial(_proj_kernel, n_lat=n_lat, n_batch=nb, tm=tm),
        grid=(nb, nt // tm, P_W // tn),
        in_specs=[pl.BlockSpec((1, tm, d), lambda b, i, j: (b, i, 0)),
                  pl.BlockSpec((8, 3 * d), lambda b, i, j: (0, 0)),
                  pl.BlockSpec((1, d), lambda b, i, j: (0, 0)),
                  pl.BlockSpec((None, d, tn), lambda b, i, j: (layer, 0, j))],
        out_specs=[pl.BlockSpec((1, tm, tn), lambda b, i, j: (b, i, j)),
                   pl.BlockSpec((1, tm, 128), lambda b, i, j: (b, i, 0))],
        out_shape=[jax.ShapeDtypeStruct((nb, nt, P_W), BF16),
                   jax.ShapeDtypeStruct((nb, nt, 128), F32)],
        scratch_shapes=[pltpu.VMEM((tm, d), BF16)],
        compiler_params=_cparams("parallel", "parallel", "arbitrary"),
        name="norm_proj",
    )(xa, mod, norm_g.reshape(1, d), w_pad)


def _aprep_kernel(qk_ref, v_ref, cos_ref, sin_ref, qt_ref, k_ref, vt_ref):
    cosv = cos_ref[...]
    sinv = sin_ref[...]
    lane = lax.broadcasted_iota(jnp.int32, (1, 128), 1)
    even = (lane % 2) == 0
    first = lane < DA_DH

    def rope(x):
        swapped = jnp.where(even, pltpu.roll(x, 127, 1), pltpu.roll(x, 1, 1))
        return x * cosv + swapped * sinv

    for h in range(DA_HEADS):
        q = rope(qk_ref[0, :, h * 128:(h + 1) * 128].astype(F32)) * (DA_DH ** -0.5 * LOG2E)
        k = rope(qk_ref[0, :, BR_W + h * 128:BR_W + (h + 1) * 128].astype(F32))
        k_ref[0, :, h * 128:(h + 1) * 128] = k.astype(BF16)
        qt_ref[0, 2 * h] = jnp.where(first, q, 0.0).T.astype(BF16)
        qt_ref[0, 2 * h + 1] = jnp.where(first, 0.0, q).T.astype(BF16)
        vt_ref[0, 0, h * 128:(h + 1) * 128, :] = v_ref[0, :, h * 128:(h + 1) * 128].astype(F32).T.astype(BF16)


def _attn_prep(p, cos_t, sin_t):
    nb, nt, _ = p.shape
    tm = ROW_TILE
    nti = nt // tm
    return pl.pallas_call(
        _aprep_kernel,
        grid=(nb, nti),
        in_specs=[pl.BlockSpec((1, tm, 2 * BR_W), lambda b, i: (b, i, 0)),
                  pl.BlockSpec((1, tm, BR_W), lambda b, i: (b, i, U_AV)),
                  pl.BlockSpec((tm, 128), lambda b, i: (i, 0)),
                  pl.BlockSpec((tm, 128), lambda b, i: (i, 0))],
        out_specs=[pl.BlockSpec((1, 2 * DA_HEADS, 128, tm), lambda b, i: (b, 0, 0, i)),
                   pl.BlockSpec((1, tm, BR_W), lambda b, i: (b, i, 0)),
                   pl.BlockSpec((1, 1, BR_W, tm), lambda b, i: (b, i, 0, 0))],
        out_shape=[jax.ShapeDtypeStruct((nb, 2 * DA_HEADS, 128, nt), BF16),
                   jax.ShapeDtypeStruct((nb, nt, BR_W), BF16),
                   jax.ShapeDtypeStruct((nb, nti, BR_W, tm), BF16)],
        compiler_params=_cparams("parallel", "parallel"),
        name="attn_prep",
    )(p, p, cos_t, sin_t)


def _lambda(lq_ref, lk_ref, lam_init):
    e = jnp.exp(jnp.sum(lq_ref[...] * lk_ref[...], axis=-1, keepdims=True))
    return e[0:1] - e[1:2] + lam_init


SUM_ROWS = 16


def _attn_kernel(lq_ref, lk_ref, g_ref, q1_ref, q2_ref, k_ref, vt_ref, z_ref, o_ref, acc1, acc2,
                 *, nkv, tk, lam_init):
    q1 = q1_ref[0, 0]
    q2 = q2_ref[0, 0]
    tq = q1.shape[1]
    acc1[...] = jnp.zeros_like(acc1)
    acc2[...] = jnp.zeros_like(acc2)
    ones = jnp.ones((SUM_ROWS, tk), BF16)

    def scores(q, j):
        s = jnp.dot(k_ref[0, j * tk:(j + 1) * tk, :], q, preferred_element_type=F32)
        return s, jnp.max(s, axis=0, keepdims=True)

    def accumulate(s_mx, acc, j, m):
        s, mx = s_mx
        vv = jnp.concatenate([vt_ref[0, j], ones], axis=0)
        mn = jnp.maximum(m, mx)
        p = jnp.exp2(s - mn).astype(BF16)
        acc[...] = acc[...] * jnp.exp2(m - mn) + jnp.dot(vv, p, preferred_element_type=F32)
        return mn

    m1 = m2 = jnp.full((1, tq), NEG, F32)
    cur1, cur2 = scores(q1, 0), scores(q2, 0)
    for j in range(nkv):
        if j + 1 < nkv:
            nxt1 = scores(q1, j + 1)
        m1 = accumulate(cur1, acc1, j, m1)
        if j + 1 < nkv:
            nxt2 = scores(q2, j + 1)
        m2 = accumulate(cur2, acc2, j, m2)
        cur1, cur2 = nxt1, nxt2

    lam = _lambda(lq_ref, lk_ref, lam_init)
    l1 = acc1[DA_DV:DA_DV + 1, :]
    l2 = acc2[DA_DV:DA_DV + 1, :]
    o = acc1[0:DA_DV, :] / l1 - lam * (acc2[0:DA_DV, :] / l2)
    ms = jnp.mean(o * o, axis=0, keepdims=True)
    y = (o * lax.rsqrt(ms + EPS)).T * g_ref[...] * (1.0 - lam_init)
    o_ref[0] = (y * _silu(z_ref[0].astype(F32))).astype(o_ref.dtype)


def _diff_attention(p, qt, kr, vt, lam_q, lam_k, da_g, lam_init, n_lat, tq):
    nb, nt, _ = p.shape
    nkv, tk = vt.shape[1], vt.shape[3]
    return pl.pallas_call(
        functools.partial(_attn_kernel, nkv=nkv, tk=tk, lam_init=lam_init),
        grid=(nb, DA_HEADS, n_lat // tq),
        in_specs=[pl.BlockSpec((2, DA_DH), lambda b, h, i: (0, 0)),
                  pl.BlockSpec((2, DA_DH), lambda b, h, i: (0, 0)),
                  pl.BlockSpec((1, DA_DV), lambda b, h, i: (0, 0)),
                  pl.BlockSpec((1, 1, 128, tq), lambda b, h, i: (b, 2 * h, 0, i)),
                  pl.BlockSpec((1, 1, 128, tq), lambda b, h, i: (b, 2 * h + 1, 0, i)),
                  pl.BlockSpec((1, nt, 128), lambda b, h, i: (b, 0, h)),
                  pl.BlockSpec((1, nkv, 128, tk), lambda b, h, i: (b, 0, h, 0)),
                  pl.BlockSpec((1, tq, 128), lambda b, h, i: (b, i, U_AZ * 4 + h))],
        out_specs=pl.BlockSpec((1, tq, 128), lambda b, h, i: (b, i, h)),
        out_shape=jax.ShapeDtypeStruct((nb, n_lat, BR_W), BF16),
        scratch_shapes=[pltpu.VMEM((DA_DV + SUM_ROWS, tq), F32), pltpu.VMEM((DA_DV + SUM_ROWS, tq), F32)],
        compiler_params=_cparams("parallel", "parallel", "parallel"),
        name="diff_attention",
    )(lam_q, lam_k, da_g.reshape(1, DA_DV), qt, qt, kr, vt, p)


HALO = 16


def _log_sigmoid(x):
    return jnp.minimum(x, 0.0) - jnp.log1p(jnp.exp(-jnp.abs(x)))


def _bprep_kernel(x_ref, prev_ref, next_ref, v_ref, g_ref, w_ref, cb_ref, gb_ref, k_ref, qt_ref, vt_ref, gl_ref,
                  *, n_lat, n_tok, tm):
    i = pl.program_id(1)
    x = x_ref[0].astype(F32)
    row = lax.broadcasted_iota(jnp.int32, (tm, 1), 0)
    pos = i * tm + row
    xp = jnp.where(row == 0, prev_ref[0, HALO - 1:HALO, :].astype(F32), pltpu.roll(x, 1, 0))
    xp = jnp.where((pos == 0) | (pos == n_lat), 0.0, xp)
    xn = jnp.where(row == tm - 1, next_ref[0, 0:1, :].astype(F32), pltpu.roll(x, tm - 1, 0))
    xn = jnp.where((pos == n_lat - 1) | (pos == n_tok - 1), 0.0, xn)
    y = xp * w_ref[0:1, :] + x * w_ref[1:2, :] + xn * w_ref[2:3, :] + cb_ref[...]
    y = _silu(y)
    k_ref[0] = (y[:, BR_W:] * (ML_DH ** -0.5)).astype(BF16)
    for c in range(tm // ML_CHUNK):
        rows = slice(c * ML_CHUNK, (c + 1) * ML_CHUNK)
        qt_ref[0, c] = y[rows, :BR_W].T.astype(BF16)
        vt_ref[0, c] = v_ref[0, rows, :].astype(F32).T.astype(BF16)
    g = g_ref[0] + gb_ref[...]
    gl_lane = lax.broadcasted_iota(jnp.int32, (1, 128), 1)
    is_f = (gl_lane % (2 * ML_HEADS)) >= ML_HEADS
    gl_ref[0] = jnp.where(is_f, _log_sigmoid(g), g)


def _mlstm_prep(p, gates, conv_w, conv_b, gate_b, n_lat):
    nb, nt, _ = p.shape
    tm = ROW_TILE
    hb = tm // HALO
    last = nt // HALO - 1
    gb = jnp.zeros((1, 128), F32).at[0, :N_GATE].set(gate_b.reshape(-1))
    cpt = tm // ML_CHUNK
    tspec = pl.BlockSpec((1, cpt, BR_W, ML_CHUNK), lambda b, i: (b, i, 0, 0))
    tshape = jax.ShapeDtypeStruct((nb, nt // ML_CHUNK, BR_W, ML_CHUNK), BF16)
    return pl.pallas_call(
        functools.partial(_bprep_kernel, n_lat=n_lat, n_tok=nt, tm=tm),
        grid=(nb, nt // tm),
        in_specs=[pl.BlockSpec((1, tm, 2 * BR_W), lambda b, i: (b, i, U_BQ // 2)),
                  pl.BlockSpec((1, HALO, 2 * BR_W), lambda b, i: (b, jnp.maximum(i * hb - 1, 0), U_BQ // 2)),
                  pl.BlockSpec((1, HALO, 2 * BR_W), lambda b, i: (b, jnp.minimum((i + 1) * hb, last), U_BQ // 2)),
                  pl.BlockSpec((1, tm, BR_W), lambda b, i: (b, i, U_BV)),
                  pl.BlockSpec((1, tm, 128), lambda b, i: (b, i, 0)),
                  pl.BlockSpec((3, 2 * BR_W), lambda b, i: (0, 0)),
                  pl.BlockSpec((1, 2 * BR_W), lambda b, i: (0, 0)),
                  pl.BlockSpec((1, 128), lambda b, i: (0, 0))],
        out_specs=[pl.BlockSpec((1, tm, BR_W), lambda b, i: (b, i, 0)), tspec, tspec,
                   pl.BlockSpec((1, tm, 128), lambda b, i: (b, i, 0))],
        out_shape=[jax.ShapeDtypeStruct((nb, nt, BR_W), BF16), tshape, tshape,
                   jax.ShapeDtypeStruct((nb, nt, 128), F32)],
        compiler_params=_cparams("parallel", "parallel"),
        name="mlstm_prep",
    )(p, p, p, p, gates, conv_w, conv_b.reshape(1, -1), gb)


def _split_dot(a, b_f32, a_is_tri):
    hi = b_f32.astype(BF16)
    lo = (b_f32 - hi.astype(F32)).astype(BF16)
    if a_is_tri:
        return jnp.dot(a, hi, preferred_element_type=F32) + jnp.dot(a, lo, preferred_element_type=F32)
    return jnp.dot(hi, a, preferred_element_type=F32) + jnp.dot(lo, a, preferred_element_type=F32)


def _scan_kernel(kf_ref, kb_ref, qtf_ref, qtb_ref, vtf_ref, vtb_ref, glf_ref, glb_ref, gtf_ref, gtb_ref,
                 hf_ref, hb_ref, c_scr, m_scr):
    t = pl.program_id(0)
    L = ML_CHUNK
    n_batch = kf_ref.shape[0]

    @pl.when(t == 0)
    def _():
        c_scr[...] = jnp.zeros_like(c_scr)
        m_scr[...] = jnp.zeros_like(m_scr)

    si = lax.broadcasted_iota(jnp.int32, (L, L), 0)
    ti = lax.broadcasted_iota(jnp.int32, (L, L), 1)
    ones_t = jnp.ones((SUM_ROWS, L), BF16)
    dirs = ((kf_ref, qtf_ref, vtf_ref, glf_ref, gtf_ref, hf_ref), (kb_ref, qtb_ref, vtb_ref, glb_ref, gtb_ref, hb_ref))
    chains = [(b, d, h) for b in range(n_batch) for d in range(2) for h in range(ML_HEADS)]
    hsl = lambda h: slice(h * ML_DH, (h + 1) * ML_DH)

    seen, gates = [], {}
    for d in range(2):
        sd = (si <= ti) if d == 0 else (si >= ti)
        tri_c = ((ti <= si) if d == 0 else (ti >= si)).astype(BF16)
        seen.append(sd)
        for b in range(n_batch):
            gl = dirs[d][3][b]
            gt = dirs[d][4][b, 0]
            gates[b, d] = (gl, _split_dot(tri_c, gl, True), _split_dot(sd.astype(BF16), gt, False))

    ks = [dirs[d][0][b, :, hsl(h)] for b, d, h in chains]
    qts = [dirs[d][1][b, 0, hsl(h), :] for b, d, h in chains]
    cmats = [c_scr[i] for i in range(len(chains))]
    raw = [jnp.dot(k, qt, preferred_element_type=F32) for k, qt in zip(ks, qts)]
    inter = [jnp.dot(c.astype(BF16), qt, preferred_element_type=F32) for c, qt in zip(cmats, qts)]

    vecs, h_loc, kv_loc = [], [], []
    for i, (b, d, h) in enumerate(chains):
        gl, bc_all, br_all = gates[b, d]
        ci = d * 2 * ML_HEADS + h
        cf = ci + ML_HEADS
        last = L - 1 if d == 0 else 0
        li_c = gl[:, ci:ci + 1]
        b_c = bc_all[:, cf:cf + 1]
        b_r = br_all[cf:cf + 1, :]
        b_tot = b_c[last:last + 1, :]
        dmat = jnp.where(seen[d], b_r + (li_c - b_c), NEG)
        rmax = jnp.max(dmat, axis=0, keepdims=True)
        s = (raw[i] * jnp.exp(dmat - rmax)).astype(BF16)
        g_c = b_tot - b_c + li_c
        g_max = jnp.max(g_c, axis=0, keepdims=True)
        wk = (ks[i].astype(F32) * jnp.exp(g_c - g_max)).astype(BF16)
        vt_aug = jnp.concatenate([dirs[d][2][b, 0, hsl(h), :], ones_t], axis=0)
        h_loc.append(jnp.dot(vt_aug, s, preferred_element_type=F32))
        kv_loc.append(jnp.dot(vt_aug, wk, preferred_element_type=F32))
        vecs.append((b_r, b_tot, rmax, g_max))

    for i, (b, d, h) in enumerate(chains):
        b_r, b_tot, rmax, g_max = vecs[i]
        m = m_scr[i][:, 0:1]
        m_new = jnp.maximum(b_tot + m, g_max)
        c_scr[i] = jnp.exp(b_tot + m - m_new) * cmats[i] + jnp.exp(g_max - m_new) * kv_loc[i]
        m_scr[i] = jnp.broadcast_to(m_new, (1, 128))
        m_inter = b_r + m
        m_t = jnp.maximum(m_inter, rmax)
        num = jnp.exp(m_inter - m_t) * inter[i] + jnp.exp(rmax - m_t) * h_loc[i]
        den = num[ML_DH:ML_DH + 1, :]
        dirs[d][5][b, 0, hsl(h), :] = num[:ML_DH, :] / jnp.maximum(jnp.abs(den), jnp.exp(-m_t))


def _mlstm_scan(kc, qt, vt, gl, glt, n_lat):
    nb, nt, _ = kc.shape
    L = ML_CHUNK
    nch = nt // L
    nlc = n_lat // L
    fwd = lambda t: (t + nlc) % nch
    bwd = lambda t: nch - 1 - t
    tspec = lambda f: pl.BlockSpec((nb, 1, BR_W, L), lambda t: (0, f(t), 0, 0))
    rspec = lambda width, f: pl.BlockSpec((nb, L, width), lambda t: (0, f(t), 0))
    gspec = lambda f: pl.BlockSpec((nb, 1, N_GATE, L), lambda t: (0, f(t), 0, 0))
    hshape = jax.ShapeDtypeStruct((nb, nch, BR_W, L), F32)
    n_chain = nb * 2 * ML_HEADS
    return pl.pallas_call(
        _scan_kernel,
        grid=(nch,),
        in_specs=[rspec(BR_W, fwd), rspec(BR_W, bwd), tspec(fwd), tspec(bwd), tspec(fwd), tspec(bwd),
                  rspec(128, fwd), rspec(128, bwd), gspec(fwd), gspec(bwd)],
        out_specs=[tspec(fwd), tspec(bwd)],
        out_shape=[hshape, hshape],
        scratch_shapes=[pltpu.VMEM((n_chain, ML_DH + SUM_ROWS, ML_DH), F32),
                        pltpu.VMEM((n_chain, 1, 128), F32)],
        compiler_params=_cparams("arbitrary"),
        name="mlstm_scan",
    )(kc, kc, qt, qt, vt, vt, gl, gl, glt, glt)


NA_RB = 4
NA_BLK = NA_RB * GRID_W
NA_SLAB = 3 * NA_RB


def _toeplitz_kernel(rpb_ref, sel_ref, ok_ref, o_ref):
    x = rpb_ref[...]
    hi = x.astype(BF16)
    r1 = x - hi.astype(F32)
    mid = r1.astype(BF16)
    lo = (r1 - mid.astype(F32)).astype(BF16)
    sel = sel_ref[...]
    y = (jnp.dot(hi, sel, preferred_element_type=F32) + jnp.dot(mid, sel, preferred_element_type=F32)
         + jnp.dot(lo, sel, preferred_element_type=F32))
    o_ref[...] = jnp.where(ok_ref[...] > 0.5, y, NEG)


def _na_bias_tables(rpb_all, rows):
    depth = rpb_all.shape[0]
    nrow, ncol = 2 * NA_KH - 1, 2 * NA_KW - 1
    c = np.arange(GRID_W)
    kcol = np.arange(GRID_W)
    cs = np.clip(c - NA_KW // 2, 0, GRID_W - NA_KW)
    ok = ((kcol[None, :] >= cs[:, None]) & (kcol[None, :] < cs[:, None] + NA_KW)).reshape(1, -1)
    cidx = np.clip(kcol[None, :] - c[:, None] + NA_KW - 1, 0, ncol - 1).reshape(-1)
    sel = np.zeros((128, GRID_W * GRID_W), np.float32)
    sel[cidx, np.arange(GRID_W * GRID_W)] = 1.0
    flat = jnp.pad(rpb_all.astype(F32).reshape(depth * NA_HEADS * nrow, ncol), ((0, 0), (0, 128 - ncol)))
    toep = pl.pallas_call(
        _toeplitz_kernel,
        out_shape=jax.ShapeDtypeStruct((flat.shape[0], GRID_W * GRID_W), F32),
        compiler_params=pltpu.CompilerParams(vmem_limit_bytes=VMEM_LIMIT),
        name="na_bias_toeplitz",
    )(flat, jnp.asarray(sel, BF16), jnp.asarray(ok, F32))
    toep = toep.reshape(depth, NA_HEADS, nrow, GRID_W, GRID_W)
    plan = []
    for r0, s0 in ((0, 0), (2 * NA_RB, NA_RB), (rows - NA_RB, rows - NA_SLAB)):
        for rr in range(NA_RB):
            r = r0 + rr
            rs = min(max(r - NA_KH // 2, 0), rows - NA_KH)
            plan.append(tuple(s0 + i - r + NA_KH - 1 if rs <= s0 + i < rs + NA_KH else -1 for i in range(NA_SLAB)))
    return pl.pallas_call(
        functools.partial(_bias_tile_kernel, plan=tuple(plan)),
        grid=(depth, NA_HEADS),
        in_specs=[pl.BlockSpec((1, 1, nrow, GRID_W, GRID_W), lambda l, h: (l, h, 0, 0, 0))],
        out_specs=pl.BlockSpec((1, 3, 1, NA_BLK, NA_SLAB * GRID_W), lambda l, h: (l, 0, h, 0, 0)),
        out_shape=jax.ShapeDtypeStruct((depth, 3, NA_HEADS, NA_BLK, NA_SLAB * GRID_W), F32),
        compiler_params=_cparams("parallel", "parallel"),
        name="na_bias_tiles",
    )(toep)


def _bias_tile_kernel(t_ref, o_ref, *, plan):
    masked = jnp.full((GRID_W, GRID_W), NEG, F32)
    for n, rows_shown in enumerate(plan):
        v, rr = divmod(n, NA_RB)
        blocks = [masked if r < 0 else t_ref[0, 0, r] for r in rows_shown]
        o_ref[0, v, 0, rr * GRID_W:(rr + 1) * GRID_W, :] = jnp.concatenate(blocks, axis=1)


def _na_kernel(q_ref, k0_ref, k1_ref, k2_ref, v0_ref, v1_ref, v2_ref, kx_ref, vx_ref, z_ref, bias_ref,
               o_ref, ks, vs):
    nb = NA_BLK
    ks[0:nb] = k0_ref[0]
    ks[nb:2 * nb] = k1_ref[0]
    ks[2 * nb:3 * nb] = k2_ref[0]
    vs[0:nb] = v0_ref[0]
    vs[nb:2 * nb] = v1_ref[0]
    vs[2 * nb:3 * nb] = v2_ref[0]
    first = lax.broadcasted_iota(jnp.int32, (1, 128), 1) < NA_DH
    nt_dims = (((1,), (1,)), ((), ()))
    outs = []
    for pr in range(NA_HEADS // 2):
        cols = slice(pr * 128, (pr + 1) * 128)
        qp = q_ref[0, :, cols].astype(F32) * (NA_DH ** -0.5)
        qq = jnp.concatenate([jnp.where(first, qp, 0.0), jnp.where(first, 0.0, qp)], axis=0).astype(BF16)
        bias = jnp.concatenate([bias_ref[0, 2 * pr], bias_ref[0, 2 * pr + 1]], axis=0)
        s_loc = lax.dot_general(qq, ks[:, cols], nt_dims, preferred_element_type=F32) + bias
        s_ctx = lax.dot_general(qq, kx_ref[0, :, cols], nt_dims, preferred_element_type=F32)
        m = jnp.maximum(jnp.max(s_loc, axis=-1, keepdims=True), jnp.max(s_ctx, axis=-1, keepdims=True))
        p_loc = jnp.exp(s_loc - m)
        p_ctx = jnp.exp(s_ctx - m)
        l = jnp.sum(p_loc, axis=-1, keepdims=True) + jnp.sum(p_ctx, axis=-1, keepdims=True)
        o = (jnp.dot(p_loc.astype(BF16), vs[:, cols], preferred_element_type=F32)
             + jnp.dot(p_ctx.astype(BF16), vx_ref[0, :, cols], preferred_element_type=F32)) / l
        outs.append(jnp.where(first, o[:nb], o[nb:]))
    o_all = jnp.concatenate(outs, axis=-1)
    o_ref[0] = (o_all * _silu(z_ref[0].astype(F32))).astype(o_ref.dtype)


def _neighbourhood_attention(p, bias_tab, layer, n_lat, n_ctx):
    nb, nt, _ = p.shape
    nblk = n_lat // NA_BLK
    blk = NA_BLK
    cblk = n_lat // n_ctx
    base = lambda i: jnp.clip(i - 1, 0, nblk - 3)
    variant = lambda i: jnp.where(i == 0, 0, jnp.where(i == nblk - 1, 2, 1))
    spec = lambda unit, off: pl.BlockSpec((1, blk, BR_W), lambda b, i: (b, base(i) + off, unit))
    own = lambda unit: pl.BlockSpec((1, blk, BR_W), lambda b, i: (b, i, unit))
    return pl.pallas_call(
        _na_kernel,
        grid=(nb, nblk),
        in_specs=[own(U_CQ),
                  spec(U_CK, 0), spec(U_CK, 1), spec(U_CK, 2),
                  spec(U_CV, 0), spec(U_CV, 1), spec(U_CV, 2),
                  pl.BlockSpec((1, n_ctx, BR_W), lambda b, i: (b, cblk, U_CK)),
                  pl.BlockSpec((1, n_ctx, BR_W), lambda b, i: (b, cblk, U_CV)),
                  own(U_CZ),
                  pl.BlockSpec((None, 1, NA_HEADS, blk, NA_SLAB * GRID_W),
                               lambda b, i: (layer, variant(i), 0, 0, 0))],
        out_specs=pl.BlockSpec((1, blk, BR_W), lambda b, i: (b, i, 0)),
        out_shape=jax.ShapeDtypeStruct((nb, n_lat, BR_W), BF16),
        scratch_shapes=[pltpu.VMEM((3 * blk, BR_W), BF16), pltpu.VMEM((3 * blk, BR_W), BF16)],
        compiler_params=_cparams("parallel", "parallel"),
        name="neighbourhood_attention",
    )(p, p, p, p, p, p, p, p, p, p, bias_tab)


def _ctx_kernel(lq_ref, lk_ref, g_ref, aqk_ref, avz_ref, cqk_ref, cvz_ref, ya_ref, yc_ref, *, lam_init):
    lam = _lambda(lq_ref, lk_ref, lam_init)
    first = lax.broadcasted_iota(jnp.int32, (1, 128), 1) < 64
    second = jnp.logical_not(first)
    nt_dims = (((1,), (1,)), ((), ()))

    def softmax(s):
        e = jnp.exp(s - jnp.max(s, axis=-1, keepdims=True))
        return e / jnp.sum(e, axis=-1, keepdims=True)

    def masked_scores(q, k, msk):
        return lax.dot_general(jnp.where(msk, q, 0.0).astype(BF16), k, nt_dims, preferred_element_type=F32)

    outs = []
    for h in range(DA_HEADS):
        cols = slice(h * 128, (h + 1) * 128)
        q = aqk_ref[0, :, cols].astype(F32) * (DA_DH ** -0.5)
        k = aqk_ref[0, :, BR_W + h * 128:BR_W + (h + 1) * 128]
        v = avz_ref[0, :, cols]
        z = avz_ref[0, :, BR_W + h * 128:BR_W + (h + 1) * 128].astype(F32)
        a = softmax(masked_scores(q, k, first)) - lam * softmax(masked_scores(q, k, second))
        o = jnp.dot(a.astype(BF16), v, preferred_element_type=F32)
        ms = jnp.mean(o * o, axis=-1, keepdims=True)
        y = o * lax.rsqrt(ms + EPS) * g_ref[...] * (1.0 - lam_init)
        outs.append(y * _silu(z))
    ya_ref[0] = jnp.concatenate(outs, axis=-1).astype(ya_ref.dtype)

    outs = []
    for pr in range(NA_HEADS // 2):
        cols = slice(pr * 128, (pr + 1) * 128)
        q = cqk_ref[0, :, cols].astype(F32) * (NA_DH ** -0.5)
        k = cqk_ref[0, :, BR_W + pr * 128:BR_W + (pr + 1) * 128]
        v = cvz_ref[0, :, cols]
        pair = [jnp.dot(softmax(masked_scores(q, k, msk)).astype(BF16), v, preferred_element_type=F32)
                for msk in (first, second)]
        outs.append(jnp.where(first, pair[0], pair[1]))
    z = cvz_ref[0, :, BR_W:].astype(F32)
    yc_ref[0] = (jnp.concatenate(outs, axis=-1) * _silu(z)).astype(yc_ref.dtype)


def _ctx_attention(p, lam_q, lam_k, da_g, lam_init, n_lat, n_ctx):
    nb = p.shape[0]
    cblk = n_lat // n_ctx
    pspec = lambda unit: pl.BlockSpec((1, n_ctx, 2 * BR_W), lambda b: (b, cblk, unit // 2))
    yspec = pl.BlockSpec((1, n_ctx, BR_W), lambda b: (b, 0, 0))
    small = lambda shape: pl.BlockSpec(shape, lambda b: (0, 0))
    yshape = jax.ShapeDtypeStruct((nb, n_ctx, BR_W), BF16)
    return pl.pallas_call(
        functools.partial(_ctx_kernel, lam_init=lam_init),
        grid=(nb,),
        in_specs=[small((2, DA_DH)), small((2, DA_DH)), small((1, DA_DV)),
                  pspec(U_AQ), pspec(U_AV), pspec(U_CQ), pspec(U_CV)],
        out_specs=[yspec, yspec],
        out_shape=[yshape, yshape],
        compiler_params=_cparams("parallel"),
        name="ctx_attention",
    )(lam_q, lam_k, da_g.reshape(1, DA_DV), p, p, p, p)


def _merge_kernel(x_ref, ya_ref, yc_ref, hf_ref, hb_ref, bo_ref, bz_ref, gm0_ref, gm1_ref, gm2_ref,
                  mod_ref, mlg_ref, fg_ref, wbr_ref, wout_ref, o_ref, *, n_batch, ctx_rows, final_norm):
    b = pl.program_id(0)
    hsum = jnp.concatenate([(hf_ref[0, c] + hb_ref[0, c]).T for c in range(hf_ref.shape[1])], axis=0)
    parts = []
    for h in range(ML_HEADS):
        hh = hsum[:, h * ML_DH:(h + 1) * ML_DH]
        ms = jnp.mean(hh * hh, axis=-1, keepdims=True)
        parts.append(hh * lax.rsqrt(ms + EPS) * mlg_ref[...])
    hn = jnp.concatenate(parts, axis=-1)
    yb = (jax.nn.sigmoid(bo_ref[0].astype(F32)) * hn * _silu(bz_ref[0].astype(F32))).astype(BF16)

    def gated(gm_ref, y, n):
        return jax.nn.sigmoid(gm_ref[0].astype(F32)) * jnp.dot(y, wbr_ref[n], preferred_element_type=F32)

    merged = gated(gm0_ref, ya_ref[0], 0) + gated(gm1_ref, yb, 1) + gated(gm2_ref, yc_ref[0], 2)
    upd = jnp.dot(merged.astype(BF16), wout_ref[...], preferred_element_type=F32)

    gate = mod_ref[n_batch:n_batch + 1, 2 * D_MODEL:] if ctx_rows else mod_ref[pl.ds(b, 1), 2 * D_MODEL:]
    xo = x_ref[0] + gate * upd
    if final_norm:
        ms = jnp.mean(xo * xo, axis=-1, keepdims=True)
        xo = xo * lax.rsqrt(ms + EPS) * fg_ref[...]
    o_ref[0] = xo


def _merge(xa, p, ya, yc, hf, hb, mod, ml_g, final_g, wbr, wout, n_rows, tm, row_off, ctx_rows, final_norm):
    nb, _, d = xa.shape
    own = lambda width: pl.BlockSpec((1, tm, width), lambda b, i: (b, i, 0))
    row = lambda width, unit: pl.BlockSpec((1, tm, width), lambda b, i: (b, i + row_off, unit))
    hspec = pl.BlockSpec((1, tm // ML_CHUNK, BR_W, ML_CHUNK), lambda b, i: (b, i + row_off, 0, 0))
    const = lambda shape: pl.BlockSpec(shape, lambda b, i: (0,) * len(shape))
    return pl.pallas_call(
        functools.partial(_merge_kernel, n_batch=nb, ctx_rows=ctx_rows, final_norm=final_norm),
        grid=(nb, n_rows // tm),
        in_specs=[row(d, 0), own(BR_W), own(BR_W), hspec, hspec,
                  row(BR_W, U_BO), row(BR_W, U_BZ),
                  row(d, U_GM // 2), row(d, U_GM // 2 + 1), row(d, U_GM // 2 + 2),
                  const((8, 3 * d)), const((1, ML_DH)), const((1, d)),
                  const((3, BR_W, d)), const((d, d))],
        out_specs=own(d),
        out_shape=jax.ShapeDtypeStruct((nb, n_rows, d), F32),
        compiler_params=_cparams("parallel", "parallel"),
        name="merge",
    )(xa, ya, yc, hf, hb, p, p, p, p, p, mod, ml_g.reshape(1, ML_DH), final_g.reshape(1, d), wbr, wout)


def _rope_tables(n_lat, n_tok):
    t = jnp.arange(n_lat, dtype=jnp.int32)
    row = (t // GRID_W).astype(F32)
    col = (t % GRID_W).astype(F32)
    inv = ROPE_BASE ** (-jnp.arange(0, DA_DH // 2, 2, dtype=F32) / (DA_DH // 2))
    ang = jnp.concatenate([row[:, None] * inv, col[:, None] * inv], axis=-1)
    cos = jnp.repeat(jnp.cos(ang), 2, axis=-1)
    sin = jnp.repeat(jnp.sin(ang), 2, axis=-1) * jnp.tile(jnp.asarray([-1.0, 1.0], F32), DA_DH // 2)
    cos = jnp.tile(cos, (1, 2))
    sin = jnp.tile(sin, (1, 2))
    cos = jnp.concatenate([cos, jnp.ones((n_tok - n_lat, 128), F32)], axis=0)
    sin = jnp.concatenate([sin, jnp.zeros((n_tok - n_lat, 128), F32)], axis=0)
    return cos, sin


def _pad_w_in(w):
    depth, d, _ = w.shape
    split = 9 * BR_W + N_GATE
    zeros = jnp.zeros((depth, d, GATE_PAD - N_GATE), w.dtype)
    return jnp.concatenate([w[:, :, :split], zeros, w[:, :, split:]], axis=2).astype(BF16)


def kernel(x, c, ctx, c_ctx, w_mod, b_mod, norm_g, w_in, da_lam_q, da_lam_k, da_norm_g, ml_conv_w, ml_conv_b,
           ml_gate_b, ml_norm_g, na_rpb, w_br, w_out, final_g):
    nb, n_lat, d = x.shape
    n_ctx = ctx.shape[1]
    nt = n_lat + n_ctx
    depth = w_mod.shape[0]
    assert d == D_MODEL and nt % ROW_TILE == 0 and n_lat % n_ctx == 0 and n_lat % NA_BLK == 0 and n_lat >= 4 * NA_BLK and nb + 1 <= 8
    tq = 1024

    cs =jnp.concatenate([c, c_ctx[None], jnp.zeros((8 - nb - 1, d), F32)], axis=0)
    mods = _modulation(cs, w_mod, b_mod)
    cos_t, sin_t = _rope_tables(n_lat, nt)
    xa = jnp.concatenate([x, ctx], axis=1)
    w_in_p = _pad_w_in(w_in)
    w_br_b = w_br.astype(BF16)
    w_out_b = w_out.astype(BF16)
    bias_tabs = _na_bias_tables(na_rpb, n_lat // GRID_W)

    for l in range(depth):
        last = l == depth - 1
        lam_init = 0.8 - 0.6 * math.exp(-0.3 * l)
        p, gates = _projection(xa, mods[l], norm_g[l], w_in_p, l, n_lat)

        qt, kr, vt = _attn_prep(p, cos_t, sin_t)
        ya = _diff_attention(p, qt, kr, vt, da_lam_q[l], da_lam_k[l], da_norm_g[l], lam_init, n_lat, tq)

        kc, qtc, vtc, gl = _mlstm_prep(p, gates, ml_conv_w[l], ml_conv_b[l], ml_gate_b[l], n_lat)
        glt = jnp.transpose(gl[:, :, :N_GATE].reshape(nb, nt // ML_CHUNK, ML_CHUNK, N_GATE), (0, 1, 3, 2))
        hf, hb = _mlstm_scan(kc, qtc, vtc, gl, glt, n_lat)

        yc = _neighbourhood_attention(p, bias_tabs, l, n_lat, n_ctx)

        merge = functools.partial(_merge, xa, p, hf=hf, hb=hb, mod=mods[l], ml_g=ml_norm_g[l], final_g=final_g,
                                  wbr=w_br_b[l], wout=w_out_b[l])
        x_lat = merge(ya=ya, yc=yc, n_rows=n_lat, tm=512, row_off=0, ctx_rows=False, final_norm=last)
        if last:
            return x_lat
        ya_c, yc_c = _ctx_attention(p, da_lam_q[l], da_lam_k[l], da_norm_g[l], lam_init, n_lat, n_ctx)
        x_ctx = merge(ya=ya_c, yc=yc_c, n_rows=n_ctx, tm=n_ctx, row_off=n_lat // n_ctx, ctx_rows=True,
                      final_norm=False)
        xa = jnp.concatenate([x_lat, x_ctx], axis=1)
```

```python
import functools
import math

import numpy as np
import jax
import jax.numpy as jnp
from jax import lax
from jax.experimental import pallas as pl
from jax.experimental.pallas import tpu as pltpu

F32 = jnp.float32
BF16 = jnp.bfloat16

D_MODEL = 1024
BR_W = 512
GRID_W = 64
EPS = 1e-6
ROPE_BASE = 10000.0
DA_HEADS, DA_DH, DA_DV = 4, 64, 128
ML_HEADS, ML_DH, ML_CHUNK = 4, 128, 64
NA_HEADS, NA_DH, NA_KH, NA_KW = 8, 64, 8, 16
N_GATE = 4 * ML_HEADS
GATE_PAD = 512
P_W = 9 * BR_W + GATE_PAD + 4 * BR_W + 3 * D_MODEL
U_AQ, U_AK, U_AV, U_AZ, U_BQ, U_BK, U_BV, U_BO, U_BZ, U_BG, U_CQ, U_CK, U_CV, U_CZ, U_GM = range(15)
NEG = -1e30
LOG2E = math.log2(math.e)
VMEM_LIMIT = 56 * 1024 * 1024
ROW_TILE = 768


def _cparams(*sem):
    return pltpu.CompilerParams(dimension_semantics=sem, vmem_limit_bytes=VMEM_LIMIT)


def _silu(x):
    return x * jax.nn.sigmoid(x)


def _mod_kernel(c_ref, w_ref, b_ref, o_ref):
    s = _silu(c_ref[...])
    o_ref[0] = jnp.dot(s.astype(BF16), w_ref[0].astype(BF16), preferred_element_type=F32) + b_ref[0]


def _modulation(cs, w_mod, b_mod):
    depth, d, d3 = w_mod.shape
    tn = 1024
    return pl.pallas_call(
        _mod_kernel,
        grid=(depth, d3 // tn),
        in_specs=[pl.BlockSpec((8, d), lambda l, j: (0, 0)),
                  pl.BlockSpec((1, d, tn), lambda l, j: (l, 0, j)),
                  pl.BlockSpec((1, 1, tn), lambda l, j: (l, 0, j))],
        out_specs=pl.BlockSpec((1, 8, tn), lambda l, j: (l, 0, j)),
        out_shape=jax.ShapeDtypeStruct((depth, 8, d3), F32),
        compiler_params=_cparams("parallel", "parallel"),
        name="modulation",
    )(cs, w_mod, b_mod.reshape(depth, 1, d3))


PROJ_TN = 1024
GATE_TILE = (U_BG * BR_W) // PROJ_TN
GATE_OFF = (U_BG * BR_W) % PROJ_TN


def _proj_kernel(x_ref, mod_ref, g_ref, w_ref, o_ref, gate_ref, h_scr, *, n_lat, n_batch, tm):
    b = pl.program_id(0)
    i = pl.program_id(1)
    j = pl.program_id(2)

    @pl.when(j == 0)
    def _():
        x = x_ref[0]
        ms = jnp.mean(x * x, axis=-1, keepdims=True)
        y = x * lax.rsqrt(ms + EPS) * g_ref[...]
        row = i * tm + lax.broadcasted_iota(jnp.int32, (tm, 1), 0)
        is_ctx = row >= n_lat
        mb = mod_ref[pl.ds(b, 1), :]
        mc = mod_ref[n_batch:n_batch + 1, :]
        shift = jnp.where(is_ctx, mc[:, :D_MODEL], mb[:, :D_MODEL])
        scale = jnp.where(is_ctx, mc[:, D_MODEL:2 * D_MODEL], mb[:, D_MODEL:2 * D_MODEL])
        h_scr[...] = (y * (1.0 + scale) + shift).astype(BF16)

    acc = jnp.dot(h_scr[...], w_ref[...], preferred_element_type=F32)
    o_ref[0] = acc.astype(BF16)

    @pl.when(j == GATE_TILE)
    def _():
        gate_ref[0] = acc[:, GATE_OFF:GATE_OFF + 128]


def _projection(xa, mod, norm_g, w_pad, layer, n_lat):
    nb, nt, d = xa.shape
    tm, tn = ROW_TILE, PROJ_TN
    return pl.pallas_call(
        functools.partial(_proj_kernel, n_lat=n_lat, n_batch=nb, tm=tm),
        grid=(nb, nt // tm, P_W // tn),
        in_specs=[pl.BlockSpec((1, tm, d), lambda b, i, j: (b, i, 0)),
                  pl.BlockSpec((8, 3 * d), lambda b, i, j: (0, 0)),
                  pl.BlockSpec((1, d), lambda b, i, j: (0, 0)),
                  pl.BlockSpec((None, d, tn), lambda b, i, j: (layer, 0, j))],
        out_specs=[pl.BlockSpec((1, tm, tn), lambda b, i, j: (b, i, j)),
                   pl.BlockSpec((1, tm, 128), lambda b, i, j: (b, i, 0))],
        out_shape=[jax.ShapeDtypeStruct((nb, nt, P_W), BF16),
                   jax.ShapeDtypeStruct((nb, nt, 128), F32)],
        scratch_shapes=[pltpu.VMEM((tm, d), BF16)],
        compiler_params=_cparams("parallel", "parallel", "arbitrary"),
        name="norm_proj",
    )(xa, mod, norm_g.reshape(1, d), w_pad)


def _aprep_kernel(qk_ref, v_ref, cos_ref, sin_ref, qt_ref, k_ref, vt_ref):
    cosv = cos_ref[...]
    sinv = sin_ref[...]
    lane = lax.broadcasted_iota(jnp.int32, (1, 128), 1)
    even = (lane % 2) == 0
    first = lane < DA_DH

    def rope(x):
        swapped = jnp.where(even, pltpu.roll(x, 127, 1), pltpu.roll(x, 1, 1))
        return x * cosv + swapped * sinv

    for h in range(DA_HEADS):
        q = rope(qk_ref[0, :, h * 128:(h + 1) * 128].astype(F32)) * (DA_DH ** -0.5 * LOG2E)
        k = rope(qk_ref[0, :, BR_W + h * 128:BR_W + (h + 1) * 128].astype(F32))
        k_ref[0, :, h * 128:(h + 1) * 128] = k.astype(BF16)
        qt_ref[0, 2 * h] = jnp.where(first, q, 0.0).T.astype(BF16)
        qt_ref[0, 2 * h + 1] = jnp.where(first, 0.0, q).T.astype(BF16)
        vt_ref[0, 0, h * 128:(h + 1) * 128, :] = v_ref[0, :, h * 128:(h + 1) * 128].astype(F32).T.astype(BF16)


def _attn_prep(p, cos_t, sin_t):
    nb, nt, _ = p.shape
    tm = ROW_TILE
    nti = nt // tm
    return pl.pallas_call(
        _aprep_kernel,
        grid=(nb, nti),
        in_specs=[pl.BlockSpec((1, tm, 2 * BR_W), lambda b, i: (b, i, 0)),
                  pl.BlockSpec((1, tm, BR_W), lambda b, i: (b, i, U_AV)),
                  pl.BlockSpec((tm, 128), lambda b, i: (i, 0)),
                  pl.BlockSpec((tm, 128), lambda b, i: (i, 0))],
        out_specs=[pl.BlockSpec((1, 2 * DA_HEADS, 128, tm), lambda b, i: (b, 0, 0, i)),
                   pl.BlockSpec((1, tm, BR_W), lambda b, i: (b, i, 0)),
                   pl.BlockSpec((1, 1, BR_W, tm), lambda b, i: (b, i, 0, 0))],
        out_shape=[jax.ShapeDtypeStruct((nb, 2 * DA_HEADS, 128, nt), BF16),
                   jax.ShapeDtypeStruct((nb, nt, BR_W), BF16),
                   jax.ShapeDtypeStruct((nb, nti, BR_W, tm), BF16)],
        compiler_params=_cparams("parallel", "parallel"),
        name="attn_prep",
    )(p, p, cos_t, sin_t)


def _lambda(lq_ref, lk_ref, lam_init):
    e = jnp.exp(jnp.sum(lq_ref[...] * lk_ref[...], axis=-1, keepdims=True))
    return e[0:1] - e[1:2] + lam_init


SUM_ROWS = 16


def _attn_kernel(lq_ref, lk_ref, g_ref, q1_ref, q2_ref, k_ref, vt_ref, z_ref, o_ref, acc1, acc2,
                 *, nkv, tk, lam_init):
    q1 = q1_ref[0, 0]
    q2 = q2_ref[0, 0]
    tq = q1.shape[1]
    acc1[...] = jnp.zeros_like(acc1)
    acc2[...] = jnp.zeros_like(acc2)
    ones = jnp.ones((SUM_ROWS, tk), BF16)

    def scores(q, j):
        s = jnp.dot(k_ref[0, j * tk:(j + 1) * tk, :], q, preferred_element_type=F32)
        return s, jnp.max(s, axis=0, keepdims=True)

    def accumulate(s_mx, acc, j, m):
        s, mx = s_mx
        vv = jnp.concatenate([vt_ref[0, j], ones], axis=0)
        mn = jnp.maximum(m, mx)
        p = jnp.exp2(s - mn).astype(BF16)
        acc[...] = acc[...] * jnp.exp2(m - mn) + jnp.dot(vv, p, preferred_element_type=F32)
        return mn

    m1 = m2 = jnp.full((1, tq), NEG, F32)
    cur1, cur2 = scores(q1, 0), scores(q2, 0)
    for j in range(nkv):
        if j + 1 < nkv:
            nxt1 = scores(q1, j + 1)
        m1 = accumulate(cur1, acc1, j, m1)
        if j + 1 < nkv:
            nxt2 = scores(q2, j + 1)
        m2 = accumulate(cur2, acc2, j, m2)
        cur1, cur2 = nxt1, nxt2

    lam = _lambda(lq_ref, lk_ref, lam_init)
    l1 = acc1[DA_DV:DA_DV + 1, :]
    l2 = acc2[DA_DV:DA_DV + 1, :]
    o = acc1[0:DA_DV, :] / l1 - lam * (acc2[0:DA_DV, :] / l2)
    ms = jnp.mean(o * o, axis=0, keepdims=True)
    y = (o * lax.rsqrt(ms + EPS)).T * g_ref[...] * (1.0 - lam_init)
    o_ref[0] = (y * _silu(z_ref[0].astype(F32))).astype(o_ref.dtype)


def _diff_attention(p, qt, kr, vt, lam_q, lam_k, da_g, lam_init, n_lat, tq):
    nb, nt, _ = p.shape
    nkv, tk = vt.shape[1], vt.shape[3]
    return pl.pallas_call(
        functools.partial(_attn_kernel, nkv=nkv, tk=tk, lam_init=lam_init),
        grid=(nb, DA_HEADS, n_lat // tq),
        in_specs=[pl.BlockSpec((2, DA_DH), lambda b, h, i: (0, 0)),
                  pl.BlockSpec((2, DA_DH), lambda b, h, i: (0, 0)),
                  pl.BlockSpec((1, DA_DV), lambda b, h, i: (0, 0)),
                  pl.BlockSpec((1, 1, 128, tq), lambda b, h, i: (b, 2 * h, 0, i)),
                  pl.BlockSpec((1, 1, 128, tq), lambda b, h, i: (b, 2 * h + 1, 0, i)),
                  pl.BlockSpec((1, nt, 128), lambda b, h, i: (b, 0, h)),
                  pl.BlockSpec((1, nkv, 128, tk), lambda b, h, i: (b, 0, h, 0)),
                  pl.BlockSpec((1, tq, 128), lambda b, h, i: (b, i, U_AZ * 4 + h))],
        out_specs=pl.BlockSpec((1, tq, 128), lambda b, h, i: (b, i, h)),
        out_shape=jax.ShapeDtypeStruct((nb, n_lat, BR_W), BF16),
        scratch_shapes=[pltpu.VMEM((DA_DV + SUM_ROWS, tq), F32), pltpu.VMEM((DA_DV + SUM_ROWS, tq), F32)],
        compiler_params=_cparams("parallel", "parallel", "parallel"),
        name="diff_attention",
    )(lam_q, lam_k, da_g.reshape(1, DA_DV), qt, qt, kr, vt, p)


HALO = 16


def _log_sigmoid(x):
    return jnp.minimum(x, 0.0) - jnp.log1p(jnp.exp(-jnp.abs(x)))


def _bprep_kernel(x_ref, prev_ref, next_ref, v_ref, g_ref, w_ref, cb_ref, gb_ref, k_ref, qt_ref, vt_ref, gl_ref,
                  glt_ref, *, n_lat, n_tok, tm):
    i = pl.program_id(1)
    x = x_ref[0].astype(F32)
    row = lax.broadcasted_iota(jnp.int32, (tm, 1), 0)
    pos = i * tm + row
    xp = jnp.where(row == 0, prev_ref[0, HALO - 1:HALO, :].astype(F32), pltpu.roll(x, 1, 0))
    xp = jnp.where((pos == 0) | (pos == n_lat), 0.0, xp)
    xn = jnp.where(row == tm - 1, next_ref[0, 0:1, :].astype(F32), pltpu.roll(x, tm - 1, 0))
    xn = jnp.where((pos == n_lat - 1) | (pos == n_tok - 1), 0.0, xn)
    y = xp * w_ref[0:1, :] + x * w_ref[1:2, :] + xn * w_ref[2:3, :] + cb_ref[...]
    y = _silu(y)
    k_ref[0] = (y[:, BR_W:] * (ML_DH ** -0.5)).astype(BF16)
    for c in range(tm // ML_CHUNK):
        rows = slice(c * ML_CHUNK, (c + 1) * ML_CHUNK)
        qt_ref[0, c] = y[rows, :BR_W].T.astype(BF16)
        vt_ref[0, c] = v_ref[0, rows, :].astype(F32).T.astype(BF16)
    g = g_ref[0] + gb_ref[...]
    gl_lane = lax.broadcasted_iota(jnp.int32, (1, 128), 1)
    is_f = (gl_lane % (2 * ML_HEADS)) >= ML_HEADS
    gl = jnp.where(is_f, _log_sigmoid(g), g)
    gl_ref[0] = gl
    for c in range(tm // ML_CHUNK):
        glt_ref[0, c] = gl[c * ML_CHUNK:(c + 1) * ML_CHUNK, :].T[:N_GATE, :]


def _mlstm_prep(p, gates, conv_w, conv_b, gate_b, n_lat):
    nb, nt, _ = p.shape
    tm = ROW_TILE
    hb = tm // HALO
    last = nt // HALO - 1
    gb = jnp.zeros((1, 128), F32).at[0, :N_GATE].set(gate_b.reshape(-1))
    cpt = tm // ML_CHUNK
    tspec = pl.BlockSpec((1, cpt, BR_W, ML_CHUNK), lambda b, i: (b, i, 0, 0))
    tshape = jax.ShapeDtypeStruct((nb, nt // ML_CHUNK, BR_W, ML_CHUNK), BF16)
    return pl.pallas_call(
        functools.partial(_bprep_kernel, n_lat=n_lat, n_tok=nt, tm=tm),
        grid=(nb, nt // tm),
        in_specs=[pl.BlockSpec((1, tm, 2 * BR_W), lambda b, i: (b, i, U_BQ // 2)),
                  pl.BlockSpec((1, HALO, 2 * BR_W), lambda b, i: (b, jnp.maximum(i * hb - 1, 0), U_BQ // 2)),
                  pl.BlockSpec((1, HALO, 2 * BR_W), lambda b, i: (b, jnp.minimum((i + 1) * hb, last), U_BQ // 2)),
                  pl.BlockSpec((1, tm, BR_W), lambda b, i: (b, i, U_BV)),
                  pl.BlockSpec((1, tm, 128), lambda b, i: (b, i, 0)),
                  pl.BlockSpec((3, 2 * BR_W), lambda b, i: (0, 0)),
                  pl.BlockSpec((1, 2 * BR_W), lambda b, i: (0, 0)),
                  pl.BlockSpec((1, 128), lambda b, i: (0, 0))],
        out_specs=[pl.BlockSpec((1, tm, BR_W), lambda b, i: (b, i, 0)), tspec, tspec,
                   pl.BlockSpec((1, tm, 128), lambda b, i: (b, i, 0)),
                   pl.BlockSpec((1, cpt, N_GATE, ML_CHUNK), lambda b, i: (b, i, 0, 0))],
        out_shape=[jax.ShapeDtypeStruct((nb, nt, BR_W), BF16), tshape, tshape,
                   jax.ShapeDtypeStruct((nb, nt, 128), F32),
                   jax.ShapeDtypeStruct((nb, nt // ML_CHUNK, N_GATE, ML_CHUNK), F32)],
        compiler_params=_cparams("parallel", "parallel"),
        name="mlstm_prep",
    )(p, p, p, p, gates, conv_w, conv_b.reshape(1, -1), gb)


def _split_dot(a, b_f32, a_is_tri):
    hi = b_f32.astype(BF16)
    lo = (b_f32 - hi.astype(F32)).astype(BF16)
    if a_is_tri:
        return jnp.dot(a, hi, preferred_element_type=F32) + jnp.dot(a, lo, preferred_element_type=F32)
    return jnp.dot(hi, a, preferred_element_type=F32) + jnp.dot(lo, a, preferred_element_type=F32)


def _scan_kernel(kf_ref, kb_ref, qtf_ref, qtb_ref, vtf_ref, vtb_ref, glf_ref, glb_ref, gtf_ref, gtb_ref,
                 hf_ref, hb_ref, c_scr, m_scr):
    t = pl.program_id(0)
    L = ML_CHUNK
    n_batch = kf_ref.shape[0]

    @pl.when(t == 0)
    def _():
        c_scr[...] = jnp.zeros_like(c_scr)
        m_scr[...] = jnp.zeros_like(m_scr)

    si = lax.broadcasted_iota(jnp.int32, (L, L), 0)
    ti = lax.broadcasted_iota(jnp.int32, (L, L), 1)
    ones_t = jnp.ones((SUM_ROWS, L), BF16)
    dirs = ((kf_ref, qtf_ref, vtf_ref, glf_ref, gtf_ref, hf_ref), (kb_ref, qtb_ref, vtb_ref, glb_ref, gtb_ref, hb_ref))
    chains = [(b, d, h) for b in range(n_batch) for d in range(2) for h in range(ML_HEADS)]
    hsl = lambda h: slice(h * ML_DH, (h + 1) * ML_DH)

    seen, gates = [], {}
    for d in range(2):
        sd = (si <= ti) if d == 0 else (si >= ti)
        tri_c = ((ti <= si) if d == 0 else (ti >= si)).astype(BF16)
        seen.append(sd)
        for b in range(n_batch):
            gl = dirs[d][3][b]
            gt = dirs[d][4][b, 0]
            gates[b, d] = (gl, _split_dot(tri_c, gl, True), _split_dot(sd.astype(BF16), gt, False))

    ks = [dirs[d][0][b, :, hsl(h)] for b, d, h in chains]
    qts = [dirs[d][1][b, 0, hsl(h), :] for b, d, h in chains]
    cmats = [c_scr[i] for i in range(len(chains))]
    raw = [jnp.dot(k, qt, preferred_element_type=F32) for k, qt in zip(ks, qts)]
    inter = [jnp.dot(c.astype(BF16), qt, preferred_element_type=F32) for c, qt in zip(cmats, qts)]

    vecs, h_loc, kv_loc = [], [], []
    for i, (b, d, h) in enumerate(chains):
        gl, bc_all, br_all = gates[b, d]
        ci = d * 2 * ML_HEADS + h
        cf = ci + ML_HEADS
        last = L - 1 if d == 0 else 0
        li_c = gl[:, ci:ci + 1]
        b_c = bc_all[:, cf:cf + 1]
        b_r = br_all[cf:cf + 1, :]
        b_tot = b_c[last:last + 1, :]
        dmat = jnp.where(seen[d], b_r + (li_c - b_c), NEG)
        rmax = jnp.max(dmat, axis=0, keepdims=True)
        s = (raw[i] * jnp.exp(dmat - rmax)).astype(BF16)
        g_c = b_tot - b_c + li_c
        g_max = jnp.max(g_c, axis=0, keepdims=True)
        wk = (ks[i].astype(F32) * jnp.exp(g_c - g_max)).astype(BF16)
        vt_aug = jnp.concatenate([dirs[d][2][b, 0, hsl(h), :], ones_t], axis=0)
        h_loc.append(jnp.dot(vt_aug, s, preferred_element_type=F32))
        kv_loc.append(jnp.dot(vt_aug, wk, preferred_element_type=F32))
        vecs.append((b_r, b_tot, rmax, g_max))

    for i, (b, d, h) in enumerate(chains):
        b_r, b_tot, rmax, g_max = vecs[i]
        m = m_scr[i][:, 0:1]
        m_new = jnp.maximum(b_tot + m, g_max)
        c_scr[i] = jnp.exp(b_tot + m - m_new) * cmats[i] + jnp.exp(g_max - m_new) * kv_loc[i]
        m_scr[i] = jnp.broadcast_to(m_new, (1, 128))
        m_inter = b_r + m
        m_t = jnp.maximum(m_inter, rmax)
        num = jnp.exp(m_inter - m_t) * inter[i] + jnp.exp(rmax - m_t) * h_loc[i]
        den = num[ML_DH:ML_DH + 1, :]
        dirs[d][5][b, 0, hsl(h), :] = num[:ML_DH, :] / jnp.maximum(jnp.abs(den), jnp.exp(-m_t))


def _mlstm_scan(kc, qt, vt, gl, glt, n_lat):
    nb, nt, _ = kc.shape
    L = ML_CHUNK
    nch = nt // L
    nlc = n_lat // L
    fwd = lambda t: (t + nlc) % nch
    bwd = lambda t: nch - 1 - t
    tspec = lambda f: pl.BlockSpec((nb, 1, BR_W, L), lambda t: (0, f(t), 0, 0))
    rspec = lambda width, f: pl.BlockSpec((nb, L, width), lambda t: (0, f(t), 0))
    gspec = lambda f: pl.BlockSpec((nb, 1, N_GATE, L), lambda t: (0, f(t), 0, 0))
    hshape = jax.ShapeDtypeStruct((nb, nch, BR_W, L), F32)
    n_chain = nb * 2 * ML_HEADS
    return pl.pallas_call(
        _scan_kernel,
        grid=(nch,),
        in_specs=[rspec(BR_W, fwd), rspec(BR_W, bwd), tspec(fwd), tspec(bwd), tspec(fwd), tspec(bwd),
                  rspec(128, fwd), rspec(128, bwd), gspec(fwd), gspec(bwd)],
        out_specs=[tspec(fwd), tspec(bwd)],
        out_shape=[hshape, hshape],
        scratch_shapes=[pltpu.VMEM((n_chain, ML_DH + SUM_ROWS, ML_DH), F32),
                        pltpu.VMEM((n_chain, 1, 128), F32)],
        compiler_params=_cparams("arbitrary"),
        name="mlstm_scan",
    )(kc, kc, qt, qt, vt, vt, gl, gl, glt, glt)


NA_RB = 4
NA_BLK = NA_RB * GRID_W
NA_SLAB = 3 * NA_RB


def _toeplitz_kernel(rpb_ref, sel_ref, ok_ref, o_ref):
    x = rpb_ref[...]
    hi = x.astype(BF16)
    r1 = x - hi.astype(F32)
    mid = r1.astype(BF16)
    lo = (r1 - mid.astype(F32)).astype(BF16)
    sel = sel_ref[...]
    y = (jnp.dot(hi, sel, preferred_element_type=F32) + jnp.dot(mid, sel, preferred_element_type=F32)
         + jnp.dot(lo, sel, preferred_element_type=F32))
    o_ref[...] = jnp.where(ok_ref[...] > 0.5, y, NEG)


def _na_bias_tables(rpb_all, rows):
    depth = rpb_all.shape[0]
    nrow, ncol = 2 * NA_KH - 1, 2 * NA_KW - 1
    c = np.arange(GRID_W)
    kcol = np.arange(GRID_W)
    cs = np.clip(c - NA_KW // 2, 0, GRID_W - NA_KW)
    ok = ((kcol[None, :] >= cs[:, None]) & (kcol[None, :] < cs[:, None] + NA_KW)).reshape(1, -1)
    cidx = np.clip(kcol[None, :] - c[:, None] + NA_KW - 1, 0, ncol - 1).reshape(-1)
    sel = np.zeros((128, GRID_W * GRID_W), np.float32)
    sel[cidx, np.arange(GRID_W * GRID_W)] = 1.0
    flat = jnp.pad(rpb_all.astype(F32).reshape(depth * NA_HEADS * nrow, ncol), ((0, 0), (0, 128 - ncol)))
    toep = pl.pallas_call(
        _toeplitz_kernel,
        out_shape=jax.ShapeDtypeStruct((flat.shape[0], GRID_W * GRID_W), F32),
        compiler_params=pltpu.CompilerParams(vmem_limit_bytes=VMEM_LIMIT),
        name="na_bias_toeplitz",
    )(flat, jnp.asarray(sel, BF16), jnp.asarray(ok, F32))
    toep = toep.reshape(depth, NA_HEADS, nrow, GRID_W, GRID_W)
    plan = []
    for r0, s0 in ((0, 0), (2 * NA_RB, NA_RB), (rows - NA_RB, rows - NA_SLAB)):
        for rr in range(NA_RB):
            r = r0 + rr
            rs = min(max(r - NA_KH // 2, 0), rows - NA_KH)
            plan.append(tuple(s0 + i - r + NA_KH - 1 if rs <= s0 + i < rs + NA_KH else -1 for i in range(NA_SLAB)))
    return pl.pallas_call(
        functools.partial(_bias_tile_kernel, plan=tuple(plan)),
        grid=(depth, NA_HEADS),
        in_specs=[pl.BlockSpec((1, 1, nrow, GRID_W, GRID_W), lambda l, h: (l, h, 0, 0, 0))],
        out_specs=pl.BlockSpec((1, 3, 1, NA_BLK, NA_SLAB * GRID_W), lambda l, h: (l, 0, h, 0, 0)),
        out_shape=jax.ShapeDtypeStruct((depth, 3, NA_HEADS, NA_BLK, NA_SLAB * GRID_W), F32),
        compiler_params=_cparams("parallel", "parallel"),
        name="na_bias_tiles",
    )(toep)


def _bias_tile_kernel(t_ref, o_ref, *, plan):
    masked = jnp.full((GRID_W, GRID_W), NEG, F32)
    for n, rows_shown in enumerate(plan):
        v, rr = divmod(n, NA_RB)
        blocks = [masked if r < 0 else t_ref[0, 0, r] for r in rows_shown]
        o_ref[0, v, 0, rr * GRID_W:(rr + 1) * GRID_W, :] = jnp.concatenate(blocks, axis=1)


def _na_kernel(q_ref, k0_ref, k1_ref, k2_ref, v0_ref, v1_ref, v2_ref, kx_ref, vx_ref, z_ref, bias_ref,
               o_ref, ks, vs):
    nb = NA_BLK
    ks[0:nb] = k0_ref[0]
    ks[nb:2 * nb] = k1_ref[0]
    ks[2 * nb:3 * nb] = k2_ref[0]
    vs[0:nb] = v0_ref[0]
    vs[nb:2 * nb] = v1_ref[0]
    vs[2 * nb:3 * nb] = v2_ref[0]
    first = lax.broadcasted_iota(jnp.int32, (1, 128), 1) < NA_DH
    nt_dims = (((1,), (1,)), ((), ()))
    outs = []
    for pr in range(NA_HEADS // 2):
        cols = slice(pr * 128, (pr + 1) * 128)
        qp = q_ref[0, :, cols].astype(F32) * (NA_DH ** -0.5)
        qq = jnp.concatenate([jnp.where(first, qp, 0.0), jnp.where(first, 0.0, qp)], axis=0).astype(BF16)
        bias = jnp.concatenate([bias_ref[0, 2 * pr], bias_ref[0, 2 * pr + 1]], axis=0)
        s_loc = lax.dot_general(qq, ks[:, cols], nt_dims, preferred_element_type=F32) + bias
        s_ctx = lax.dot_general(qq, kx_ref[0, :, cols], nt_dims, preferred_element_type=F32)
        m = jnp.maximum(jnp.max(s_loc, axis=-1, keepdims=True), jnp.max(s_ctx, axis=-1, keepdims=True))
        p_loc = jnp.exp(s_loc - m)
        p_ctx = jnp.exp(s_ctx - m)
        l = jnp.sum(p_loc, axis=-1, keepdims=True) + jnp.sum(p_ctx, axis=-1, keepdims=True)
        o = (jnp.dot(p_loc.astype(BF16), vs[:, cols], preferred_element_type=F32)
             + jnp.dot(p_ctx.astype(BF16), vx_ref[0, :, cols], preferred_element_type=F32)) / l
        outs.append(jnp.where(first, o[:nb], o[nb:]))
    o_all = jnp.concatenate(outs, axis=-1)
    o_ref[0] = (o_all * _silu(z_ref[0].astype(F32))).astype(o_ref.dtype)


def _neighbourhood_attention(p, bias_tab, layer, n_lat, n_ctx):
    nb, nt, _ = p.shape
    nblk = n_lat // NA_BLK
    blk = NA_BLK
    cblk = n_lat // n_ctx
    base = lambda i: jnp.clip(i - 1, 0, nblk - 3)
    variant = lambda i: jnp.where(i == 0, 0, jnp.where(i == nblk - 1, 2, 1))
    spec = lambda unit, off: pl.BlockSpec((1, blk, BR_W), lambda b, i: (b, base(i) + off, unit))
    own = lambda unit: pl.BlockSpec((1, blk, BR_W), lambda b, i: (b, i, unit))
    return pl.pallas_call(
        _na_kernel,
        grid=(nb, nblk),
        in_specs=[own(U_CQ),
                  spec(U_CK, 0), spec(U_CK, 1), spec(U_CK, 2),
                  spec(U_CV, 0), spec(U_CV, 1), spec(U_CV, 2),
                  pl.BlockSpec((1, n_ctx, BR_W), lambda b, i: (b, cblk, U_CK)),
                  pl.BlockSpec((1, n_ctx, BR_W), lambda b, i: (b, cblk, U_CV)),
                  own(U_CZ),
                  pl.BlockSpec((None, 1, NA_HEADS, blk, NA_SLAB * GRID_W),
                               lambda b, i: (layer, variant(i), 0, 0, 0))],
        out_specs=pl.BlockSpec((1, blk, BR_W), lambda b, i: (b, i, 0)),
        out_shape=jax.ShapeDtypeStruct((nb, n_lat, BR_W), BF16),
        scratch_shapes=[pltpu.VMEM((3 * blk, BR_W), BF16), pltpu.VMEM((3 * blk, BR_W), BF16)],
        compiler_params=_cparams("parallel", "parallel"),
        name="neighbourhood_attention",
    )(p, p, p, p, p, p, p, p, p, p, bias_tab)


def _ctx_kernel(lq_ref, lk_ref, g_ref, aqk_ref, avz_ref, cqk_ref, cvz_ref, ya_ref, yc_ref, *, lam_init):
    lam = _lambda(lq_ref, lk_ref, lam_init)
    first = lax.broadcasted_iota(jnp.int32, (1, 128), 1) < 64
    second = jnp.logical_not(first)
    nt_dims = (((1,), (1,)), ((), ()))

    def softmax(s):
        e = jnp.exp(s - jnp.max(s, axis=-1, keepdims=True))
        return e / jnp.sum(e, axis=-1, keepdims=True)

    def masked_scores(q, k, msk):
        return lax.dot_general(jnp.where(msk, q, 0.0).astype(BF16), k, nt_dims, preferred_element_type=F32)

    outs = []
    for h in range(DA_HEADS):
        cols = slice(h * 128, (h + 1) * 128)
        q = aqk_ref[0, :, cols].astype(F32) * (DA_DH ** -0.5)
        k = aqk_ref[0, :, BR_W + h * 128:BR_W + (h + 1) * 128]
        v = avz_ref[0, :, cols]
        z = avz_ref[0, :, BR_W + h * 128:BR_W + (h + 1) * 128].astype(F32)
        a = softmax(masked_scores(q, k, first)) - lam * softmax(masked_scores(q, k, second))
        o = jnp.dot(a.astype(BF16), v, preferred_element_type=F32)
        ms = jnp.mean(o * o, axis=-1, keepdims=True)
        y = o * lax.rsqrt(ms + EPS) * g_ref[...] * (1.0 - lam_init)
        outs.append(y * _silu(z))
    ya_ref[0] = jnp.concatenate(outs, axis=-1).astype(ya_ref.dtype)

    outs = []
    for pr in range(NA_HEADS // 2):
        cols = slice(pr * 128, (pr + 1) * 128)
        q = cqk_ref[0, :, cols].astype(F32) * (NA_DH ** -0.5)
        k = cqk_ref[0, :, BR_W + pr * 128:BR_W + (pr + 1) * 128]
        v = cvz_ref[0, :, cols]
        pair = [jnp.dot(softmax(masked_scores(q, k, msk)).astype(BF16), v, preferred_element_type=F32)
                for msk in (first, second)]
        outs.append(jnp.where(first, pair[0], pair[1]))
    z = cvz_ref[0, :, BR_W:].astype(F32)
    yc_ref[0] = (jnp.concatenate(outs, axis=-1) * _silu(z)).astype(yc_ref.dtype)


def _ctx_attention(p, lam_q, lam_k, da_g, lam_init, n_lat, n_ctx):
    nb = p.shape[0]
    cblk = n_lat // n_ctx
    pspec = lambda unit: pl.BlockSpec((1, n_ctx, 2 * BR_W), lambda b: (b, cblk, unit // 2))
    yspec = pl.BlockSpec((1, n_ctx, BR_W), lambda b: (b, 0, 0))
    small = lambda shape: pl.BlockSpec(shape, lambda b: (0, 0))
    yshape = jax.ShapeDtypeStruct((nb, n_ctx, BR_W), BF16)
    return pl.pallas_call(
        functools.partial(_ctx_kernel, lam_init=lam_init),
        grid=(nb,),
        in_specs=[small((2, DA_DH)), small((2, DA_DH)), small((1, DA_DV)),
                  pspec(U_AQ), pspec(U_AV), pspec(U_CQ), pspec(U_CV)],
        out_specs=[yspec, yspec],
        out_shape=[yshape, yshape],
        compiler_params=_cparams("parallel"),
        name="ctx_attention",
    )(lam_q, lam_k, da_g.reshape(1, DA_DV), p, p, p, p)


def _merge_kernel(x_ref, ya_ref, yc_ref, hf_ref, hb_ref, bo_ref, bz_ref, gm0_ref, gm1_ref, gm2_ref,
                  mod_ref, mlg_ref, fg_ref, wbr_ref, wout_ref, o_ref, *, n_batch, ctx_rows, final_norm):
    b = pl.program_id(0)
    hsum = jnp.concatenate([(hf_ref[0, c] + hb_ref[0, c]).T for c in range(hf_ref.shape[1])], axis=0)
    parts = []
    for h in range(ML_HEADS):
        hh = hsum[:, h * ML_DH:(h + 1) * ML_DH]
        ms = jnp.mean(hh * hh, axis=-1, keepdims=True)
        parts.append(hh * lax.rsqrt(ms + EPS) * mlg_ref[...])
    hn = jnp.concatenate(parts, axis=-1)
    yb = (jax.nn.sigmoid(bo_ref[0].astype(F32)) * hn * _silu(bz_ref[0].astype(F32))).astype(BF16)

    def gated(gm_ref, y, n):
        return jax.nn.sigmoid(gm_ref[0].astype(F32)) * jnp.dot(y, wbr_ref[n], preferred_element_type=F32)

    merged = gated(gm0_ref, ya_ref[0], 0) + gated(gm1_ref, yb, 1) + gated(gm2_ref, yc_ref[0], 2)
    upd = jnp.dot(merged.astype(BF16), wout_ref[...], preferred_element_type=F32)

    gate = mod_ref[n_batch:n_batch + 1, 2 * D_MODEL:] if ctx_rows else mod_ref[pl.ds(b, 1), 2 * D_MODEL:]
    xo = x_ref[0] + gate * upd
    if final_norm:
        ms = jnp.mean(xo * xo, axis=-1, keepdims=True)
        xo = xo * lax.rsqrt(ms + EPS) * fg_ref[...]
    o_ref[0] = xo


def _merge(xa, p, ya, yc, hf, hb, mod, ml_g, final_g, wbr, wout, n_rows, tm, row_off, ctx_rows, final_norm):
    nb, _, d = xa.shape
    own = lambda width: pl.BlockSpec((1, tm, width), lambda b, i: (b, i, 0))
    row = lambda width, unit: pl.BlockSpec((1, tm, width), lambda b, i: (b, i + row_off, unit))
    hspec = pl.BlockSpec((1, tm // ML_CHUNK, BR_W, ML_CHUNK), lambda b, i: (b, i + row_off, 0, 0))
    const = lambda shape: pl.BlockSpec(shape, lambda b, i: (0,) * len(shape))
    return pl.pallas_call(
        functools.partial(_merge_kernel, n_batch=nb, ctx_rows=ctx_rows, final_norm=final_norm),
        grid=(nb, n_rows // tm),
        in_specs=[row(d, 0), own(BR_W), own(BR_W), hspec, hspec,
                  row(BR_W, U_BO), row(BR_W, U_BZ),
                  row(d, U_GM // 2), row(d, U_GM // 2 + 1), row(d, U_GM // 2 + 2),
                  const((8, 3 * d)), const((1, ML_DH)), const((1, d)),
                  const((3, BR_W, d)), const((d, d))],
        out_specs=own(d),
        out_shape=jax.ShapeDtypeStruct((nb, n_rows, d), F32),
        compiler_params=_cparams("parallel", "parallel"),
        name="merge",
    )(xa, ya, yc, hf, hb, p, p, p, p, p, mod, ml_g.reshape(1, ML_DH), final_g.reshape(1, d), wbr, wout)


def _rope_tables(n_lat, n_tok):
    t = jnp.arange(n_lat, dtype=jnp.int32)
    row = (t // GRID_W).astype(F32)
    col = (t % GRID_W).astype(F32)
    inv = ROPE_BASE ** (-jnp.arange(0, DA_DH // 2, 2, dtype=F32) / (DA_DH // 2))
    ang = jnp.concatenate([row[:, None] * inv, col[:, None] * inv], axis=-1)
    cos = jnp.repeat(jnp.cos(ang), 2, axis=-1)
    sin = jnp.repeat(jnp.sin(ang), 2, axis=-1) * jnp.tile(jnp.asarray([-1.0, 1.0], F32), DA_DH // 2)
    cos = jnp.tile(cos, (1, 2))
    sin = jnp.tile(sin, (1, 2))
    cos = jnp.concatenate([cos, jnp.ones((n_tok - n_lat, 128), F32)], axis=0)
    sin = jnp.concatenate([sin, jnp.zeros((n_tok - n_lat, 128), F32)], axis=0)
    return cos, sin


def _pad_w_in(w):
    depth, d, _ = w.shape
    split = 9 * BR_W + N_GATE
    zeros = jnp.zeros((depth, d, GATE_PAD - N_GATE), w.dtype)
    return jnp.concatenate([w[:, :, :split], zeros, w[:, :, split:]], axis=2).astype(BF16)


def kernel(x, c, ctx, c_ctx, w_mod, b_mod, norm_g, w_in, da_lam_q, da_lam_k, da_norm_g, ml_conv_w, ml_conv_b,
           ml_gate_b, ml_norm_g, na_rpb, w_br, w_out, final_g):
    nb, n_lat, d = x.shape
    n_ctx = ctx.shape[1]
    nt = n_lat + n_ctx
    depth = w_mod.shape[0]
    assert d == D_MODEL and nt % ROW_TILE == 0 and n_lat % n_ctx == 0 and n_lat % NA_BLK == 0 and n_lat >= 4 * NA_BLK and nb + 1 <= 8
    tq = 1024

    cs =jnp.concatenate([c, c_ctx[None], jnp.zeros((8 - nb - 1, d), F32)], axis=0)
    mods = _modulation(cs, w_mod, b_mod)
    cos_t, sin_t = _rope_tables(n_lat, nt)
    xa = jnp.concatenate([x, ctx], axis=1)
    w_in_p = _pad_w_in(w_in)
    w_br_b = w_br.astype(BF16)
    w_out_b = w_out.astype(BF16)
    bias_tabs = _na_bias_tables(na_rpb, n_lat // GRID_W)

    for l in range(depth):
        last = l == depth - 1
        lam_init = 0.8 - 0.6 * math.exp(-0.3 * l)
        p, gates = _projection(xa, mods[l], norm_g[l], w_in_p, l, n_lat)

        qt, kr, vt = _attn_prep(p, cos_t, sin_t)
        ya = _diff_attention(p, qt, kr, vt, da_lam_q[l], da_lam_k[l], da_norm_g[l], lam_init, n_lat, tq)

        kc, qtc, vtc, gl, glt = _mlstm_prep(p, gates, ml_conv_w[l], ml_conv_b[l], ml_gate_b[l], n_lat)
        hf, hb = _mlstm_scan(kc, qtc, vtc, gl, glt, n_lat)

        yc = _neighbourhood_attention(p, bias_tabs, l, n_lat, n_ctx)

        merge = functools.partial(_merge, xa, p, hf=hf, hb=hb, mod=mods[l], ml_g=ml_norm_g[l], final_g=final_g,
                                  wbr=w_br_b[l], wout=w_out_b[l])
        x_lat = merge(ya=ya, yc=yc, n_rows=n_lat, tm=512, row_off=0, ctx_rows=False, final_norm=last)
        if last:
            return x_lat
        ya_c, yc_c = _ctx_attention(p, da_lam_q[l], da_lam_k[l], da_norm_g[l], lam_init, n_lat, n_ctx)
        x_ctx = merge(ya=ya_c, yc=yc_c, n_rows=n_ctx, tm=n_ctx, row_off=n_lat // n_ctx, ctx_rows=True,
                      final_norm=False)
        xa = jnp.concatenate([x_lat, x_ctx], axis=1)
```

```python
import functools
import math

import numpy as np
import jax
import jax.numpy as jnp
from jax import lax
from jax.experimental import pallas as pl
from jax.experimental.pallas import tpu as pltpu

F32 = jnp.float32
BF16 = jnp.bfloat16

D_MODEL = 1024
BR_W = 512
GRID_W = 64
EPS = 1e-6
ROPE_BASE = 10000.0
DA_HEADS, DA_DH, DA_DV = 4, 64, 128
ML_HEADS, ML_DH, ML_CHUNK = 4, 128, 64
NA_HEADS, NA_DH, NA_KH, NA_KW = 8, 64, 8, 16
N_GATE = 4 * ML_HEADS
GATE_PAD = 512
P_W = 9 * BR_W + GATE_PAD + 4 * BR_W + 3 * D_MODEL
U_AQ, U_AK, U_AV, U_AZ, U_BQ, U_BK, U_BV, U_BO, U_BZ, U_BG, U_CQ, U_CK, U_CV, U_CZ, U_GM = range(15)
NEG = -1e30
LOG2E = math.log2(math.e)
VMEM_LIMIT = 56 * 1024 * 1024
ROW_TILE = 768


def _cparams(*sem):
    return pltpu.CompilerParams(dimension_semantics=sem, vmem_limit_bytes=VMEM_LIMIT)


def _silu(x):
    return x * jax.nn.sigmoid(x)


def _mod_kernel(c_ref, w_ref, b_ref, o_ref):
    s = _silu(c_ref[...])
    o_ref[0] = jnp.dot(s.astype(BF16), w_ref[0].astype(BF16), preferred_element_type=F32) + b_ref[0]


def _modulation(cs, w_mod, b_mod):
    depth, d, d3 = w_mod.shape
    tn = 1024
    return pl.pallas_call(
        _mod_kernel,
        grid=(depth, d3 // tn),
        in_specs=[pl.BlockSpec((8, d), lambda l, j: (0, 0)),
                  pl.BlockSpec((1, d, tn), lambda l, j: (l, 0, j)),
                  pl.BlockSpec((1, 1, tn), lambda l, j: (l, 0, j))],
        out_specs=pl.BlockSpec((1, 8, tn), lambda l, j: (l, 0, j)),
        out_shape=jax.ShapeDtypeStruct((depth, 8, d3), F32),
        compiler_params=_cparams("parallel", "parallel"),
        name="modulation",
    )(cs, w_mod, b_mod.reshape(depth, 1, d3))


PROJ_TN = 1024
GATE_TILE = (U_BG * BR_W) // PROJ_TN
GATE_OFF = (U_BG * BR_W) % PROJ_TN


def _proj_kernel(x_ref, mod_ref, g_ref, w_ref, o_ref, gate_ref, h_scr, *, n_lat, n_batch, tm):
    b = pl.program_id(0)
    i = pl.program_id(1)
    j = pl.program_id(2)

    @pl.when(j == 0)
    def _():
        x = x_ref[0]
        ms = jnp.mean(x * x, axis=-1, keepdims=True)
        y = x * lax.rsqrt(ms + EPS) * g_ref[...]
        row = i * tm + lax.broadcasted_iota(jnp.int32, (tm, 1), 0)
        is_ctx = row >= n_lat
        mb = mod_ref[pl.ds(b, 1), :]
        mc = mod_ref[n_batch:n_batch + 1, :]
        shift = jnp.where(is_ctx, mc[:, :D_MODEL], mb[:, :D_MODEL])
        scale = jnp.where(is_ctx, mc[:, D_MODEL:2 * D_MODEL], mb[:, D_MODEL:2 * D_MODEL])
        h_scr[...] = (y * (1.0 + scale) + shift).astype(BF16)

    acc = jnp.dot(h_scr[...], w_ref[...], preferred_element_type=F32)
    o_ref[0] = acc.astype(BF16)

    @pl.when(j == GATE_TILE)
    def _():
        gate_ref[0] = acc[:, GATE_OFF:GATE_OFF + 128]


def _projection(xa, mod, norm_g, w_pad, layer, n_lat):
    nb, nt, d = xa.shape
    tm, tn = ROW_TILE, PROJ_TN
    return pl.pallas_call(
        functools.partial(_proj_kernel, n_lat=n_lat, n_batch=nb, tm=tm),
        grid=(nb, nt // tm, P_W // tn),
        in_specs=[pl.BlockSpec((1, tm, d), lambda b, i, j: (b, i, 0)),
                  pl.BlockSpec((8, 3 * d), lambda b, i, j: (0, 0)),
                  pl.BlockSpec((1, d), lambda b, i, j: (0, 0)),
                  pl.BlockSpec((None, d, tn), lambda b, i, j: (layer, 0, j))],
        out_specs=[pl.BlockSpec((1, tm, tn), lambda b, i, j: (b, i, j)),
                   pl.BlockSpec((1, tm, 128), lambda b, i, j: (b, i, 0))],
        out_shape=[jax.ShapeDtypeStruct((nb, nt, P_W), BF16),
                   jax.ShapeDtypeStruct((nb, nt, 128), F32)],
        scratch_shapes=[pltpu.VMEM((tm, d), BF16)],
        compiler_params=_cparams("parallel", "parallel", "arbitrary"),
        name="norm_proj",
    )(xa, mod, norm_g.reshape(1, d), w_pad)


def _aprep_kernel(qk_ref, v_ref, cos_ref, sin_ref, qt_ref, k_ref, vt_ref):
    cosv = cos_ref[...]
    sinv = sin_ref[...]
    lane = lax.broadcasted_iota(jnp.int32, (1, 128), 1)
    even = (lane % 2) == 0
    first = lane < DA_DH

    def rope(x):
        swapped = jnp.where(even, pltpu.roll(x, 127, 1), pltpu.roll(x, 1, 1))
        return x * cosv + swapped * sinv

    for h in range(DA_HEADS):
        q = rope(qk_ref[0, :, h * 128:(h + 1) * 128].astype(F32)) * (DA_DH ** -0.5 * LOG2E)
        k = rope(qk_ref[0, :, BR_W + h * 128:BR_W + (h + 1) * 128].astype(F32))
        k_ref[0, :, h * 128:(h + 1) * 128] = k.astype(BF16)
        qt_ref[0, 2 * h] = jnp.where(first, q, 0.0).T.astype(BF16)
        qt_ref[0, 2 * h + 1] = jnp.where(first, 0.0, q).T.astype(BF16)
        vt_ref[0, 0, h * 128:(h + 1) * 128, :] = v_ref[0, :, h * 128:(h + 1) * 128].astype(F32).T.astype(BF16)


def _attn_prep(p, cos_t, sin_t):
    nb, nt, _ = p.shape
    tm = ROW_TILE
    nti = nt // tm
    return pl.pallas_call(
        _aprep_kernel,
        grid=(nb, nti),
        in_specs=[pl.BlockSpec((1, tm, 2 * BR_W), lambda b, i: (b, i, 0)),
                  pl.BlockSpec((1, tm, BR_W), lambda b, i: (b, i, U_AV)),
                  pl.BlockSpec((tm, 128), lambda b, i: (i, 0)),
                  pl.BlockSpec((tm, 128), lambda b, i: (i, 0))],
        out_specs=[pl.BlockSpec((1, 2 * DA_HEADS, 128, tm), lambda b, i: (b, 0, 0, i)),
                   pl.BlockSpec((1, tm, BR_W), lambda b, i: (b, i, 0)),
                   pl.BlockSpec((1, 1, BR_W, tm), lambda b, i: (b, i, 0, 0))],
        out_shape=[jax.ShapeDtypeStruct((nb, 2 * DA_HEADS, 128, nt), BF16),
                   jax.ShapeDtypeStruct((nb, nt, BR_W), BF16),
                   jax.ShapeDtypeStruct((nb, nti, BR_W, tm), BF16)],
        compiler_params=_cparams("parallel", "parallel"),
        name="attn_prep",
    )(p, p, cos_t, sin_t)


def _lambda(lq_ref, lk_ref, lam_init):
    e = jnp.exp(jnp.sum(lq_ref[...] * lk_ref[...], axis=-1, keepdims=True))
    return e[0:1] - e[1:2] + lam_init


SUM_ROWS = 16


def _attn_kernel(lq_ref, lk_ref, g_ref, q1_ref, q2_ref, k_ref, vt_ref, z_ref, o_ref, acc1, acc2,
                 *, nkv, tk, lam_init):
    q1 = q1_ref[0, 0]
    q2 = q2_ref[0, 0]
    tq = q1.shape[1]
    acc1[...] = jnp.zeros_like(acc1)
    acc2[...] = jnp.zeros_like(acc2)
    ones = jnp.ones((SUM_ROWS, tk), BF16)

    def scores(q, j):
        s = jnp.dot(k_ref[0, j * tk:(j + 1) * tk, :], q, preferred_element_type=F32)
        return s, jnp.max(s, axis=0, keepdims=True)

    def accumulate(s_mx, acc, j, m):
        s, mx = s_mx
        vv = jnp.concatenate([vt_ref[0, j], ones], axis=0)
        mn = jnp.maximum(m, mx)
        p = jnp.exp2(s - mn).astype(BF16)
        acc[...] = acc[...] * jnp.exp2(m - mn) + jnp.dot(vv, p, preferred_element_type=F32)
        return mn

    m1 = m2 = jnp.full((1, tq), NEG, F32)
    cur1, cur2 = scores(q1, 0), scores(q2, 0)
    for j in range(nkv):
        if j + 1 < nkv:
            nxt1 = scores(q1, j + 1)
        m1 = accumulate(cur1, acc1, j, m1)
        if j + 1 < nkv:
            nxt2 = scores(q2, j + 1)
        m2 = accumulate(cur2, acc2, j, m2)
        cur1, cur2 = nxt1, nxt2

    lam = _lambda(lq_ref, lk_ref, lam_init)
    l1 = acc1[DA_DV:DA_DV + 1, :]
    l2 = acc2[DA_DV:DA_DV + 1, :]
    o = acc1[0:DA_DV, :] / l1 - lam * (acc2[0:DA_DV, :] / l2)
    ms = jnp.mean(o * o, axis=0, keepdims=True)
    y = (o * lax.rsqrt(ms + EPS)).T * g_ref[...] * (1.0 - lam_init)
    o_ref[0] = (y * _silu(z_ref[0].astype(F32))).astype(o_ref.dtype)


def _diff_attention(p, qt, kr, vt, lam_q, lam_k, da_g, lam_init, n_lat, tq):
    nb, nt, _ = p.shape
    nkv, tk = vt.shape[1], vt.shape[3]
    return pl.pallas_call(
        functools.partial(_attn_kernel, nkv=nkv, tk=tk, lam_init=lam_init),
        grid=(nb, DA_HEADS, n_lat // tq),
        in_specs=[pl.BlockSpec((2, DA_DH), lambda b, h, i: (0, 0)),
                  pl.BlockSpec((2, DA_DH), lambda b, h, i: (0, 0)),
                  pl.BlockSpec((1, DA_DV), lambda b, h, i: (0, 0)),
                  pl.BlockSpec((1, 1, 128, tq), lambda b, h, i: (b, 2 * h, 0, i)),
                  pl.BlockSpec((1, 1, 128, tq), lambda b, h, i: (b, 2 * h + 1, 0, i)),
                  pl.BlockSpec((1, nt, 128), lambda b, h, i: (b, 0, h)),
                  pl.BlockSpec((1, nkv, 128, tk), lambda b, h, i: (b, 0, h, 0)),
                  pl.BlockSpec((1, tq, 128), lambda b, h, i: (b, i, U_AZ * 4 + h))],
        out_specs=pl.BlockSpec((1, tq, 128), lambda b, h, i: (b, i, h)),
        out_shape=jax.ShapeDtypeStruct((nb, n_lat, BR_W), BF16),
        scratch_shapes=[pltpu.VMEM((DA_DV + SUM_ROWS, tq), F32), pltpu.VMEM((DA_DV + SUM_ROWS, tq), F32)],
        compiler_params=_cparams("parallel", "parallel", "parallel"),
        name="diff_attention",
    )(lam_q, lam_k, da_g.reshape(1, DA_DV), qt, qt, kr, vt, p)


HALO = 16


def _log_sigmoid(x):
    return jnp.minimum(x, 0.0) - jnp.log1p(jnp.exp(-jnp.abs(x)))


def _bprep_kernel(x_ref, prev_ref, next_ref, v_ref, g_ref, w_ref, cb_ref, gb_ref, k_ref, qt_ref, vt_ref, gl_ref,
                  glt_ref, *, n_lat, n_tok, tm):
    i = pl.program_id(1)
    x = x_ref[0].astype(F32)
    row = lax.broadcasted_iota(jnp.int32, (tm, 1), 0)
    pos = i * tm + row
    xp = jnp.where(row == 0, prev_ref[0, HALO - 1:HALO, :].astype(F32), pltpu.roll(x, 1, 0))
    xp = jnp.where((pos == 0) | (pos == n_lat), 0.0, xp)
    xn = jnp.where(row == tm - 1, next_ref[0, 0:1, :].astype(F32), pltpu.roll(x, tm - 1, 0))
    xn = jnp.where((pos == n_lat - 1) | (pos == n_tok - 1), 0.0, xn)
    y = xp * w_ref[0:1, :] + x * w_ref[1:2, :] + xn * w_ref[2:3, :] + cb_ref[...]
    y = _silu(y)
    k_ref[0] = (y[:, BR_W:] * (ML_DH ** -0.5)).astype(BF16)
    for c in range(tm // ML_CHUNK):
        rows = slice(c * ML_CHUNK, (c + 1) * ML_CHUNK)
        qt_ref[0, c] = y[rows, :BR_W].T.astype(BF16)
        vt_ref[0, c] = v_ref[0, rows, :].astype(F32).T.astype(BF16)
    g = g_ref[0] + gb_ref[...]
    gl_lane = lax.broadcasted_iota(jnp.int32, (1, 128), 1)
    is_f = (gl_lane % (2 * ML_HEADS)) >= ML_HEADS
    gl = jnp.where(is_f, _log_sigmoid(g), g)
    gl_ref[0] = gl
    for c in range(tm // ML_CHUNK):
        glt_ref[0, c] = gl[c * ML_CHUNK:(c + 1) * ML_CHUNK, :].T[:N_GATE, :]


def _mlstm_prep(p, gates, conv_w, conv_b, gate_b, n_lat):
    nb, nt, _ = p.shape
    tm = ROW_TILE
    hb = tm // HALO
    last = nt // HALO - 1
    gb = jnp.zeros((1, 128), F32).at[0, :N_GATE].set(gate_b.reshape(-1))
    cpt = tm // ML_CHUNK
    tspec = pl.BlockSpec((1, cpt, BR_W, ML_CHUNK), lambda b, i: (b, i, 0, 0))
    tshape = jax.ShapeDtypeStruct((nb, nt // ML_CHUNK, BR_W, ML_CHUNK), BF16)
    return pl.pallas_call(
        functools.partial(_bprep_kernel, n_lat=n_lat, n_tok=nt, tm=tm),
        grid=(nb, nt // tm),
        in_specs=[pl.BlockSpec((1, tm, 2 * BR_W), lambda b, i: (b, i, U_BQ // 2)),
                  pl.BlockSpec((1, HALO, 2 * BR_W), lambda b, i: (b, jnp.maximum(i * hb - 1, 0), U_BQ // 2)),
                  pl.BlockSpec((1, HALO, 2 * BR_W), lambda b, i: (b, jnp.minimum((i + 1) * hb, last), U_BQ // 2)),
                  pl.BlockSpec((1, tm, BR_W), lambda b, i: (b, i, U_BV)),
                  pl.BlockSpec((1, tm, 128), lambda b, i: (b, i, 0)),
                  pl.BlockSpec((3, 2 * BR_W), lambda b, i: (0, 0)),
                  pl.BlockSpec((1, 2 * BR_W), lambda b, i: (0, 0)),
                  pl.BlockSpec((1, 128), lambda b, i: (0, 0))],
        out_specs=[pl.BlockSpec((1, tm, BR_W), lambda b, i: (b, i, 0)), tspec, tspec,
                   pl.BlockSpec((1, tm, 128), lambda b, i: (b, i, 0)),
                   pl.BlockSpec((1, cpt, N_GATE, ML_CHUNK), lambda b, i: (b, i, 0, 0))],
        out_shape=[jax.ShapeDtypeStruct((nb, nt, BR_W), BF16), tshape, tshape,
                   jax.ShapeDtypeStruct((nb, nt, 128), F32),
                   jax.ShapeDtypeStruct((nb, nt // ML_CHUNK, N_GATE, ML_CHUNK), F32)],
        compiler_params=_cparams("parallel", "parallel"),
        name="mlstm_prep",
    )(p, p, p, p, gates, conv_w, conv_b.reshape(1, -1), gb)


def _split_dot(a, b_f32, a_is_tri):
    hi = b_f32.astype(BF16)
    lo = (b_f32 - hi.astype(F32)).astype(BF16)
    if a_is_tri:
        return jnp.dot(a, hi, preferred_element_type=F32) + jnp.dot(a, lo, preferred_element_type=F32)
    return jnp.dot(hi, a, preferred_element_type=F32) + jnp.dot(lo, a, preferred_element_type=F32)


def _scan_kernel(kf_ref, kb_ref, qtf_ref, qtb_ref, vtf_ref, vtb_ref, glf_ref, glb_ref, gtf_ref, gtb_ref,
                 hf_ref, hb_ref, c_scr, m_scr):
    t = pl.program_id(0)
    L = ML_CHUNK
    n_batch = kf_ref.shape[0]

    @pl.when(t == 0)
    def _():
        c_scr[...] = jnp.zeros_like(c_scr)
        m_scr[...] = jnp.zeros_like(m_scr)

    si = lax.broadcasted_iota(jnp.int32, (L, L), 0)
    ti = lax.broadcasted_iota(jnp.int32, (L, L), 1)
    ones_t = jnp.ones((SUM_ROWS, L), BF16)
    dirs = ((kf_ref, qtf_ref, vtf_ref, glf_ref, gtf_ref, hf_ref), (kb_ref, qtb_ref, vtb_ref, glb_ref, gtb_ref, hb_ref))
    chains = [(b, d, h) for b in range(n_batch) for d in range(2) for h in range(ML_HEADS)]
    hsl = lambda h: slice(h * ML_DH, (h + 1) * ML_DH)

    seen, gates = [], {}
    for d in range(2):
        sd = (si <= ti) if d == 0 else (si >= ti)
        tri_c = ((ti <= si) if d == 0 else (ti >= si)).astype(BF16)
        seen.append(sd)
        for b in range(n_batch):
            gl = dirs[d][3][b]
            gt = dirs[d][4][b, 0]
            gates[b, d] = (gl, _split_dot(tri_c, gl, True), _split_dot(sd.astype(BF16), gt, False))

    ks = [dirs[d][0][b, :, hsl(h)] for b, d, h in chains]
    qts = [dirs[d][1][b, 0, hsl(h), :] for b, d, h in chains]
    cmats = [c_scr[i] for i in range(len(chains))]
    raw = [jnp.dot(k, qt, preferred_element_type=F32) for k, qt in zip(ks, qts)]
    inter = [jnp.dot(c.astype(BF16), qt, preferred_element_type=F32) for c, qt in zip(cmats, qts)]

    vecs, h_loc, kv_loc = [], [], []
    for i, (b, d, h) in enumerate(chains):
        gl, bc_all, br_all = gates[b, d]
        ci = d * 2 * ML_HEADS + h
        cf = ci + ML_HEADS
        last = L - 1 if d == 0 else 0
        li_c = gl[:, ci:ci + 1]
        b_c = bc_all[:, cf:cf + 1]
        b_r = br_all[cf:cf + 1, :]
        b_tot = b_c[last:last + 1, :]
        dmat = jnp.where(seen[d], b_r + (li_c - b_c), NEG)
        rmax = jnp.max(dmat, axis=0, keepdims=True)
        s = (raw[i] * jnp.exp(dmat - rmax)).astype(BF16)
        g_c = b_tot - b_c + li_c
        g_max = jnp.max(g_c, axis=0, keepdims=True)
        wk = (ks[i].astype(F32) * jnp.exp(g_c - g_max)).astype(BF16)
        vt_aug = jnp.concatenate([dirs[d][2][b, 0, hsl(h), :], ones_t], axis=0)
        h_loc.append(jnp.dot(vt_aug, s, preferred_element_type=F32))
        kv_loc.append(jnp.dot(vt_aug, wk, preferred_element_type=F32))
        vecs.append((b_r, b_tot, rmax, g_max))

    for i, (b, d, h) in enumerate(chains):
        b_r, b_tot, rmax, g_max = vecs[i]
        m = m_scr[i][:, 0:1]
        m_new = jnp.maximum(b_tot + m, g_max)
        c_scr[i] = jnp.exp(b_tot + m - m_new) * cmats[i] + jnp.exp(g_max - m_new) * kv_loc[i]
        m_scr[i] = jnp.broadcast_to(m_new, (1, 128))
        m_inter = b_r + m
        m_t = jnp.maximum(m_inter, rmax)
        num = jnp.exp(m_inter - m_t) * inter[i] + jnp.exp(rmax - m_t) * h_loc[i]
        den = num[ML_DH:ML_DH + 1, :]
        dirs[d][5][b, 0, hsl(h), :] = num[:ML_DH, :] / jnp.maximum(jnp.abs(den), jnp.exp(-m_t))


def _mlstm_scan(kc, qt, vt, gl, glt, n_lat):
    nb, nt, _ = kc.shape
    L = ML_CHUNK
    nch = nt // L
    nlc = n_lat // L
    fwd = lambda t: (t + nlc) % nch
    bwd = lambda t: nch - 1 - t
    tspec = lambda f: pl.BlockSpec((nb, 1, BR_W, L), lambda t: (0, f(t), 0, 0))
    rspec = lambda width, f: pl.BlockSpec((nb, L, width), lambda t: (0, f(t), 0))
    gspec = lambda f: pl.BlockSpec((nb, 1, N_GATE, L), lambda t: (0, f(t), 0, 0))
    hshape = jax.ShapeDtypeStruct((nb, nch, BR_W, L), F32)
    n_chain = nb * 2 * ML_HEADS
    return pl.pallas_call(
        _scan_kernel,
        grid=(nch,),
        in_specs=[rspec(BR_W, fwd), rspec(BR_W, bwd), tspec(fwd), tspec(bwd), tspec(fwd), tspec(bwd),
                  rspec(128, fwd), rspec(128, bwd), gspec(fwd), gspec(bwd)],
        out_specs=[tspec(fwd), tspec(bwd)],
        out_shape=[hshape, hshape],
        scratch_shapes=[pltpu.VMEM((n_chain, ML_DH + SUM_ROWS, ML_DH), F32),
                        pltpu.VMEM((n_chain, 1, 128), F32)],
        compiler_params=_cparams("arbitrary"),
        name="mlstm_scan",
    )(kc, kc, qt, qt, vt, vt, gl, gl, glt, glt)


NA_RB = 4
NA_BLK = NA_RB * GRID_W
NA_SLAB = 3 * NA_RB


def _toeplitz_kernel(rpb_ref, sel_ref, ok_ref, o_ref):
    x = rpb_ref[...]
    hi = x.astype(BF16)
    r1 = x - hi.astype(F32)
    mid = r1.astype(BF16)
    lo = (r1 - mid.astype(F32)).astype(BF16)
    sel = sel_ref[...]
    y = (jnp.dot(hi, sel, preferred_element_type=F32) + jnp.dot(mid, sel, preferred_element_type=F32)
         + jnp.dot(lo, sel, preferred_element_type=F32))
    o_ref[...] = jnp.where(ok_ref[...] > 0.5, y, NEG)


def _na_bias_tables(rpb_all, rows):
    depth = rpb_all.shape[0]
    nrow, ncol = 2 * NA_KH - 1, 2 * NA_KW - 1
    c = np.arange(GRID_W)
    kcol = np.arange(GRID_W)
    cs = np.clip(c - NA_KW // 2, 0, GRID_W - NA_KW)
    ok = ((kcol[None, :] >= cs[:, None]) & (kcol[None, :] < cs[:, None] + NA_KW)).reshape(1, -1)
    cidx = np.clip(kcol[None, :] - c[:, None] + NA_KW - 1, 0, ncol - 1).reshape(-1)
    sel = np.zeros((128, GRID_W * GRID_W), np.float32)
    sel[cidx, np.arange(GRID_W * GRID_W)] = 1.0
    flat = jnp.pad(rpb_all.astype(F32).reshape(depth * NA_HEADS * nrow, ncol), ((0, 0), (0, 128 - ncol)))
    toep = pl.pallas_call(
        _toeplitz_kernel,
        out_shape=jax.ShapeDtypeStruct((flat.shape[0], GRID_W * GRID_W), F32),
        compiler_params=pltpu.CompilerParams(vmem_limit_bytes=VMEM_LIMIT),
        name="na_bias_toeplitz",
    )(flat, jnp.asarray(sel, BF16), jnp.asarray(ok, F32))
    toep = toep.reshape(depth, NA_HEADS, nrow, GRID_W, GRID_W)
    plan = []
    for r0, s0 in ((0, 0), (2 * NA_RB, NA_RB), (rows - NA_RB, rows - NA_SLAB)):
        for rr in range(NA_RB):
            r = r0 + rr
            rs = min(max(r - NA_KH // 2, 0), rows - NA_KH)
            plan.append(tuple(s0 + i - r + NA_KH - 1 if rs <= s0 + i < rs + NA_KH else -1 for i in range(NA_SLAB)))
    return pl.pallas_call(
        functools.partial(_bias_tile_kernel, plan=tuple(plan)),
        grid=(depth, NA_HEADS),
        in_specs=[pl.BlockSpec((1, 1, nrow, GRID_W, GRID_W), lambda l, h: (l, h, 0, 0, 0))],
        out_specs=pl.BlockSpec((1, 3, 1, NA_BLK, NA_SLAB * GRID_W), lambda l, h: (l, 0, h, 0, 0)),
        out_shape=jax.ShapeDtypeStruct((depth, 3, NA_HEADS, NA_BLK, NA_SLAB * GRID_W), F32),
        compiler_params=_cparams("parallel", "parallel"),
        name="na_bias_tiles",
    )(toep)


def _bias_tile_kernel(t_ref, o_ref, *, plan):
    masked = jnp.full((GRID_W, GRID_W), NEG, F32)
    for n, rows_shown in enumerate(plan):
        v, rr = divmod(n, NA_RB)
        blocks = [masked if r < 0 else t_ref[0, 0, r] for r in rows_shown]
        o_ref[0, v, 0, rr * GRID_W:(rr + 1) * GRID_W, :] = jnp.concatenate(blocks, axis=1)


def _na_kernel(q_ref, k0_ref, k1_ref, k2_ref, v0_ref, v1_ref, v2_ref, kx_ref, vx_ref, z_ref, bias_ref,
               o_ref, ks, vs):
    nb = NA_BLK
    ks[0:nb] = k0_ref[0]
    ks[nb:2 * nb] = k1_ref[0]
    ks[2 * nb:3 * nb] = k2_ref[0]
    vs[0:nb] = v0_ref[0]
    vs[nb:2 * nb] = v1_ref[0]
    vs[2 * nb:3 * nb] = v2_ref[0]
    first = lax.broadcasted_iota(jnp.int32, (1, 128), 1) < NA_DH
    nt_dims = (((1,), (1,)), ((), ()))
    outs = []
    for pr in range(NA_HEADS // 2):
        cols = slice(pr * 128, (pr + 1) * 128)
        qp = q_ref[0, :, cols].astype(F32) * (NA_DH ** -0.5)
        qq = jnp.concatenate([jnp.where(first, qp, 0.0), jnp.where(first, 0.0, qp)], axis=0).astype(BF16)
        bias = jnp.concatenate([bias_ref[0, 2 * pr], bias_ref[0, 2 * pr + 1]], axis=0)
        s_loc = lax.dot_general(qq, ks[:, cols], nt_dims, preferred_element_type=F32) + bias
        s_ctx = lax.dot_general(qq, kx_ref[0, :, cols], nt_dims, preferred_element_type=F32)
        m = jnp.maximum(jnp.max(s_loc, axis=-1, keepdims=True), jnp.max(s_ctx, axis=-1, keepdims=True))
        p_loc = jnp.exp(s_loc - m)
        p_ctx = jnp.exp(s_ctx - m)
        l = jnp.sum(p_loc, axis=-1, keepdims=True) + jnp.sum(p_ctx, axis=-1, keepdims=True)
        o = (jnp.dot(p_loc.astype(BF16), vs[:, cols], preferred_element_type=F32)
             + jnp.dot(p_ctx.astype(BF16), vx_ref[0, :, cols], preferred_element_type=F32)) / l
        outs.append(jnp.where(first, o[:nb], o[nb:]))
    o_all = jnp.concatenate(outs, axis=-1)
    o_ref[0] = (o_all * _silu(z_ref[0].astype(F32))).astype(o_ref.dtype)


def _neighbourhood_attention(p, bias_tab, layer, n_lat, n_ctx):
    nb, nt, _ = p.shape
    nblk = n_lat // NA_BLK
    blk = NA_BLK
    cblk = n_lat // n_ctx
    base = lambda i: jnp.clip(i - 1, 0, nblk - 3)
    variant = lambda i: jnp.where(i == 0, 0, jnp.where(i == nblk - 1, 2, 1))
    spec = lambda unit, off: pl.BlockSpec((1, blk, BR_W), lambda b, i: (b, base(i) + off, unit))
    own = lambda unit: pl.BlockSpec((1, blk, BR_W), lambda b, i: (b, i, unit))
    return pl.pallas_call(
        _na_kernel,
        grid=(nb, nblk),
        in_specs=[own(U_CQ),
                  spec(U_CK, 0), spec(U_CK, 1), spec(U_CK, 2),
                  spec(U_CV, 0), spec(U_CV, 1), spec(U_CV, 2),
                  pl.BlockSpec((1, n_ctx, BR_W), lambda b, i: (b, cblk, U_CK)),
                  pl.BlockSpec((1, n_ctx, BR_W), lambda b, i: (b, cblk, U_CV)),
                  own(U_CZ),
                  pl.BlockSpec((None, 1, NA_HEADS, blk, NA_SLAB * GRID_W),
                               lambda b, i: (layer, variant(i), 0, 0, 0))],
        out_specs=pl.BlockSpec((1, blk, BR_W), lambda b, i: (b, i, 0)),
        out_shape=jax.ShapeDtypeStruct((nb, n_lat, BR_W), BF16),
        scratch_shapes=[pltpu.VMEM((3 * blk, BR_W), BF16), pltpu.VMEM((3 * blk, BR_W), BF16)],
        compiler_params=_cparams("parallel", "parallel"),
        name="neighbourhood_attention",
    )(p, p, p, p, p, p, p, p, p, p, bias_tab)


def _ctx_kernel(lq_ref, lk_ref, g_ref, aqk_ref, avz_ref, cqk_ref, cvz_ref, ya_ref, yc_ref, *, lam_init):
    lam = _lambda(lq_ref, lk_ref, lam_init)
    first = lax.broadcasted_iota(jnp.int32, (1, 128), 1) < 64
    second = jnp.logical_not(first)
    nt_dims = (((1,), (1,)), ((), ()))

    def softmax(s):
        e = jnp.exp(s - jnp.max(s, axis=-1, keepdims=True))
        return e / jnp.sum(e, axis=-1, keepdims=True)

    def masked_scores(q, k, msk):
        return lax.dot_general(jnp.where(msk, q, 0.0).astype(BF16), k, nt_dims, preferred_element_type=F32)

    outs = []
    for h in range(DA_HEADS):
        cols = slice(h * 128, (h + 1) * 128)
        q = aqk_ref[0, :, cols].astype(F32) * (DA_DH ** -0.5)
        k = aqk_ref[0, :, BR_W + h * 128:BR_W + (h + 1) * 128]
        v = avz_ref[0, :, cols]
        z = avz_ref[0, :, BR_W + h * 128:BR_W + (h + 1) * 128].astype(F32)
        a = softmax(masked_scores(q, k, first)) - lam * softmax(masked_scores(q, k, second))
        o = jnp.dot(a.astype(BF16), v, preferred_element_type=F32)
        ms = jnp.mean(o * o, axis=-1, keepdims=True)
        y = o * lax.rsqrt(ms + EPS) * g_ref[...] * (1.0 - lam_init)
        outs.append(y * _silu(z))
    ya_ref[0] = jnp.concatenate(outs, axis=-1).astype(ya_ref.dtype)

    outs = []
    for pr in range(NA_HEADS // 2):
        cols = slice(pr * 128, (pr + 1) * 128)
        q = cqk_ref[0, :, cols].astype(F32) * (NA_DH ** -0.5)
        k = cqk_ref[0, :, BR_W + pr * 128:BR_W + (pr + 1) * 128]
        v = cvz_ref[0, :, cols]
        pair = [jnp.dot(softmax(masked_scores(q, k, msk)).astype(BF16), v, preferred_element_type=F32)
                for msk in (first, second)]
        outs.append(jnp.where(first, pair[0], pair[1]))
    z = cvz_ref[0, :, BR_W:].astype(F32)
    yc_ref[0] = (jnp.concatenate(outs, axis=-1) * _silu(z)).astype(yc_ref.dtype)


def _ctx_attention(p, lam_q, lam_k, da_g, lam_init, n_lat, n_ctx):
    nb = p.shape[0]
    cblk = n_lat // n_ctx
    pspec = lambda unit: pl.BlockSpec((1, n_ctx, 2 * BR_W), lambda b: (b, cblk, unit // 2))
    yspec = pl.BlockSpec((1, n_ctx, BR_W), lambda b: (b, 0, 0))
    small = lambda shape: pl.BlockSpec(shape, lambda b: (0, 0))
    yshape = jax.ShapeDtypeStruct((nb, n_ctx, BR_W), BF16)
    return pl.pallas_call(
        functools.partial(_ctx_kernel, lam_init=lam_init),
        grid=(nb,),
        in_specs=[small((2, DA_DH)), small((2, DA_DH)), small((1, DA_DV)),
                  pspec(U_AQ), pspec(U_AV), pspec(U_CQ), pspec(U_CV)],
        out_specs=[yspec, yspec],
        out_shape=[yshape, yshape],
        compiler_params=_cparams("parallel"),
        name="ctx_attention",
    )(lam_q, lam_k, da_g.reshape(1, DA_DV), p, p, p, p)


def _merge_kernel(x_ref, ya_ref, yc_ref, hf_ref, hb_ref, bo_ref, bz_ref, gm0_ref, gm1_ref, gm2_ref,
                  mod_ref, mlg_ref, fg_ref, wbr_ref, wout_ref, o_ref, *, n_batch, ctx_rows, final_norm):
    b = pl.program_id(0)
    hsum = jnp.concatenate([(hf_ref[0, c] + hb_ref[0, c]).T for c in range(hf_ref.shape[1])], axis=0)
    parts = []
    for h in range(ML_HEADS):
        hh = hsum[:, h * ML_DH:(h + 1) * ML_DH]
        ms = jnp.mean(hh * hh, axis=-1, keepdims=True)
        parts.append(hh * lax.rsqrt(ms + EPS) * mlg_ref[...])
    hn = jnp.concatenate(parts, axis=-1)
    yb = (jax.nn.sigmoid(bo_ref[0].astype(F32)) * hn * _silu(bz_ref[0].astype(F32))).astype(BF16)

    def gated(gm_ref, y, n):
        return jax.nn.sigmoid(gm_ref[0].astype(F32)) * jnp.dot(y, wbr_ref[n], preferred_element_type=F32)

    merged = gated(gm0_ref, ya_ref[0], 0) + gated(gm1_ref, yb, 1) + gated(gm2_ref, yc_ref[0], 2)
    upd = jnp.dot(merged.astype(BF16), wout_ref[...], preferred_element_type=F32)

    gate = mod_ref[n_batch:n_batch + 1, 2 * D_MODEL:] if ctx_rows else mod_ref[pl.ds(b, 1), 2 * D_MODEL:]
    xo = x_ref[0] + gate * upd
    if final_norm:
        ms = jnp.mean(xo * xo, axis=-1, keepdims=True)
        xo = xo * lax.rsqrt(ms + EPS) * fg_ref[...]
    o_ref[0] = xo


def _merge(xa, p, ya, yc, hf, hb, mod, ml_g, final_g, wbr, wout, n_rows, tm, row_off, ctx_rows, final_norm):
    nb, _, d = xa.shape
    own = lambda width: pl.BlockSpec((1, tm, width), lambda b, i: (b, i, 0))
    row = lambda width, unit: pl.BlockSpec((1, tm, width), lambda b, i: (b, i + row_off, unit))
    hspec = pl.BlockSpec((1, tm // ML_CHUNK, BR_W, ML_CHUNK), lambda b, i: (b, i + row_off, 0, 0))
    const = lambda shape: pl.BlockSpec(shape, lambda b, i: (0,) * len(shape))
    return pl.pallas_call(
        functools.partial(_merge_kernel, n_batch=nb, ctx_rows=ctx_rows, final_norm=final_norm),
        grid=(nb, n_rows // tm),
        in_specs=[row(d, 0), own(BR_W), own(BR_W), hspec, hspec,
                  row(BR_W, U_BO), row(BR_W, U_BZ),
                  row(d, U_GM // 2), row(d, U_GM // 2 + 1), row(d, U_GM // 2 + 2),
                  const((8, 3 * d)), const((1, ML_DH)), const((1, d)),
                  const((3, BR_W, d)), const((d, d))],
        out_specs=own(d),
        out_shape=jax.ShapeDtypeStruct((nb, n_rows, d), F32),
        compiler_params=_cparams("parallel", "parallel"),
        name="merge",
    )(xa, ya, yc, hf, hb, p, p, p, p, p, mod, ml_g.reshape(1, ML_DH), final_g.reshape(1, d), wbr, wout)


def _rope_tables(n_lat, n_tok):
    t = jnp.arange(n_lat, dtype=jnp.int32)
    row = (t // GRID_W).astype(F32)
    col = (t % GRID_W).astype(F32)
    inv = ROPE_BASE ** (-jnp.arange(0, DA_DH // 2, 2, dtype=F32) / (DA_DH // 2))
    ang = jnp.concatenate([row[:, None] * inv, col[:, None] * inv], axis=-1)
    cos = jnp.repeat(jnp.cos(ang), 2, axis=-1)
    sin = jnp.repeat(jnp.sin(ang), 2, axis=-1) * jnp.tile(jnp.asarray([-1.0, 1.0], F32), DA_DH // 2)
    cos = jnp.tile(cos, (1, 2))
    sin = jnp.tile(sin, (1, 2))
    cos = jnp.concatenate([cos, jnp.ones((n_tok - n_lat, 128), F32)], axis=0)
    sin = jnp.concatenate([sin, jnp.zeros((n_tok - n_lat, 128), F32)], axis=0)
    return cos, sin


def _w_layout_kernel(w_ref, o_ref):
    split = 9 * BR_W + N_GATE
    resume = split + GATE_PAD - N_GATE
    rows = w_ref.shape[1]
    o_ref[0, :, :split] = w_ref[0, :, :split].astype(BF16)
    o_ref[0, :, split:resume] = jnp.zeros((rows, resume - split), BF16)
    o_ref[0, :, resume:] = w_ref[0, :, split:].astype(BF16)


def _pad_w_in(w):
    depth, d, n_in = w.shape
    tr = 256
    return pl.pallas_call(
        _w_layout_kernel,
        grid=(depth, d // tr),
        in_specs=[pl.BlockSpec((1, tr, n_in), lambda l, i: (l, i, 0))],
        out_specs=pl.BlockSpec((1, tr, P_W), lambda l, i: (l, i, 0)),
        out_shape=jax.ShapeDtypeStruct((depth, d, P_W), BF16),
        compiler_params=_cparams("parallel", "parallel"),
        name="w_in_layout",
    )(w)


def kernel(x, c, ctx, c_ctx, w_mod, b_mod, norm_g, w_in, da_lam_q, da_lam_k, da_norm_g, ml_conv_w, ml_conv_b,
           ml_gate_b, ml_norm_g, na_rpb, w_br, w_out, final_g):
    nb, n_lat, d = x.shape
    n_ctx = ctx.shape[1]
    nt = n_lat + n_ctx
    depth = w_mod.shape[0]
    assert d == D_MODEL and nt % ROW_TILE == 0 and n_lat % n_ctx == 0 and n_lat % NA_BLK == 0 and n_lat >= 4 * NA_BLK and nb + 1 <= 8
    tq = 1024

    cs =jnp.concatenate([c, c_ctx[None], jnp.zeros((8 - nb - 1, d), F32)], axis=0)
    mods = _modulation(cs, w_mod, b_mod)
    cos_t, sin_t = _rope_tables(n_lat, nt)
    xa = jnp.concatenate([x, ctx], axis=1)
    w_in_p = _pad_w_in(w_in)
    w_br_b = w_br.astype(BF16)
    w_out_b = w_out.astype(BF16)
    bias_tabs = _na_bias_tables(na_rpb, n_lat // GRID_W)

    for l in range(depth):
        last = l == depth - 1
        lam_init = 0.8 - 0.6 * math.exp(-0.3 * l)
        p, gates = _projection(xa, mods[l], norm_g[l], w_in_p, l, n_lat)

        qt, kr, vt = _attn_prep(p, cos_t, sin_t)
        ya = _diff_attention(p, qt, kr, vt, da_lam_q[l], da_lam_k[l], da_norm_g[l], lam_init, n_lat, tq)

        kc, qtc, vtc, gl, glt = _mlstm_prep(p, gates, ml_conv_w[l], ml_conv_b[l], ml_gate_b[l], n_lat)
        hf, hb = _mlstm_scan(kc, qtc, vtc, gl, glt, n_lat)

        yc = _neighbourhood_attention(p, bias_tabs, l, n_lat, n_ctx)

        merge = functools.partial(_merge, xa, p, hf=hf, hb=hb, mod=mods[l], ml_g=ml_norm_g[l], final_g=final_g,
                                  wbr=w_br_b[l], wout=w_out_b[l])
        x_lat = merge(ya=ya, yc=yc, n_rows=n_lat, tm=512, row_off=0, ctx_rows=False, final_norm=last)
        if last:
            return x_lat
        ya_c, yc_c = _ctx_attention(p, da_lam_q[l], da_lam_k[l], da_norm_g[l], lam_init, n_lat, n_ctx)
        x_ctx = merge(ya=ya_c, yc=yc_c, n_rows=n_ctx, tm=n_ctx, row_off=n_lat // n_ctx, ctx_rows=True,
                      final_norm=False)
        xa = jnp.concatenate([x_lat, x_ctx], axis=1)
```

```python
import functools
import math

import numpy as np
import jax
import jax.numpy as jnp
from jax import lax
from jax.experimental import pallas as pl
from jax.experimental.pallas import tpu as pltpu

F32 = jnp.float32
BF16 = jnp.bfloat16

D_MODEL = 1024
BR_W = 512
GRID_W = 64
EPS = 1e-6
ROPE_BASE = 10000.0
DA_HEADS, DA_DH, DA_DV = 4, 64, 128
ML_HEADS, ML_DH, ML_CHUNK = 4, 128, 64
NA_HEADS, NA_DH, NA_KH, NA_KW = 8, 64, 8, 16
N_GATE = 4 * ML_HEADS
GATE_PAD = 512
P_W = 9 * BR_W + GATE_PAD + 4 * BR_W + 3 * D_MODEL
U_AQ, U_AK, U_AV, U_AZ, U_BQ, U_BK, U_BV, U_BO, U_BZ, U_BG, U_CQ, U_CK, U_CV, U_CZ, U_GM = range(15)
NEG = -1e30
LOG2E = math.log2(math.e)
VMEM_LIMIT = 56 * 1024 * 1024
ROW_TILE = 768


def _cparams(*sem):
    return pltpu.CompilerParams(dimension_semantics=sem, vmem_limit_bytes=VMEM_LIMIT)


def _silu(x):
    return x * jax.nn.sigmoid(x)


def _mod_kernel(c_ref, w_ref, b_ref, o_ref):
    s = _silu(c_ref[...])
    o_ref[0] = jnp.dot(s.astype(BF16), w_ref[0].astype(BF16), preferred_element_type=F32) + b_ref[0]


def _modulation(cs, w_mod, b_mod):
    depth, d, d3 = w_mod.shape
    tn = 1024
    return pl.pallas_call(
        _mod_kernel,
        grid=(depth, d3 // tn),
        in_specs=[pl.BlockSpec((8, d), lambda l, j: (0, 0)),
                  pl.BlockSpec((1, d, tn), lambda l, j: (l, 0, j)),
                  pl.BlockSpec((1, 1, tn), lambda l, j: (l, 0, j))],
        out_specs=pl.BlockSpec((1, 8, tn), lambda l, j: (l, 0, j)),
        out_shape=jax.ShapeDtypeStruct((depth, 8, d3), F32),
        compiler_params=_cparams("parallel", "parallel"),
        name="modulation",
    )(cs, w_mod, b_mod.reshape(depth, 1, d3))


PROJ_TN = 1024
GATE_TILE = (U_BG * BR_W) // PROJ_TN
GATE_OFF = (U_BG * BR_W) % PROJ_TN


def _proj_kernel(x_ref, mod_ref, g_ref, w_ref, o_ref, gate_ref, h_scr, *, n_lat, n_batch, tm):
    b = pl.program_id(0)
    i = pl.program_id(1)
    j = pl.program_id(2)

    @pl.when(j == 0)
    def _():
        x = x_ref[0]
        ms = jnp.mean(x * x, axis=-1, keepdims=True)
        y = x * lax.rsqrt(ms + EPS) * g_ref[...]
        row = i * tm + lax.broadcasted_iota(jnp.int32, (tm, 1), 0)
        is_ctx = row >= n_lat
        mb = mod_ref[pl.ds(b, 1), :]
        mc = mod_ref[n_batch:n_batch + 1, :]
        shift = jnp.where(is_ctx, mc[:, :D_MODEL], mb[:, :D_MODEL])
        scale = jnp.where(is_ctx, mc[:, D_MODEL:2 * D_MODEL], mb[:, D_MODEL:2 * D_MODEL])
        h_scr[...] = (y * (1.0 + scale) + shift).astype(BF16)

    acc = jnp.dot(h_scr[...], w_ref[...], preferred_element_type=F32)
    o_ref[0] = acc.astype(BF16)

    @pl.when(j == GATE_TILE)
    def _():
        gate_ref[0] = acc[:, GATE_OFF:GATE_OFF + 128]


def _projection(xa, mod, norm_g, w_pad, layer, n_lat):
    nb, nt, d = xa.shape
    tm, tn = ROW_TILE, PROJ_TN
    return pl.pallas_call(
        functools.partial(_proj_kernel, n_lat=n_lat, n_batch=nb, tm=tm),
        grid=(nb, nt // tm, P_W // tn),
        in_specs=[pl.BlockSpec((1, tm, d), lambda b, i, j: (b, i, 0)),
                  pl.BlockSpec((8, 3 * d), lambda b, i, j: (0, 0)),
                  pl.BlockSpec((1, d), lambda b, i, j: (0, 0)),
                  pl.BlockSpec((None, d, tn), lambda b, i, j: (layer, 0, j))],
        out_specs=[pl.BlockSpec((1, tm, tn), lambda b, i, j: (b, i, j)),
                   pl.BlockSpec((1, tm, 128), lambda b, i, j: (b, i, 0))],
        out_shape=[jax.ShapeDtypeStruct((nb, nt, P_W), BF16),
                   jax.ShapeDtypeStruct((nb, nt, 128), F32)],
        scratch_shapes=[pltpu.VMEM((tm, d), BF16)],
        compiler_params=_cparams("parallel", "parallel", "arbitrary"),
        name="norm_proj",
    )(xa, mod, norm_g.reshape(1, d), w_pad)


def _aprep_kernel(qk_ref, v_ref, cos_ref, sin_ref, qt_ref, k_ref, vt_ref):
    cosv = cos_ref[...]
    sinv = sin_ref[...]
    lane = lax.broadcasted_iota(jnp.int32, (1, 128), 1)
    even = (lane % 2) == 0
    first = lane < DA_DH

    def rope(x):
        swapped = jnp.where(even, pltpu.roll(x, 127, 1), pltpu.roll(x, 1, 1))
        return x * cosv + swapped * sinv

    for h in range(DA_HEADS):
        q = rope(qk_ref[0, :, h * 128:(h + 1) * 128].astype(F32)) * (DA_DH ** -0.5 * LOG2E)
        k = rope(qk_ref[0, :, BR_W + h * 128:BR_W + (h + 1) * 128].astype(F32))
        k_ref[0, :, h * 128:(h + 1) * 128] = k.astype(BF16)
        qt_ref[0, 2 * h] = jnp.where(first, q, 0.0).T.astype(BF16)
        qt_ref[0, 2 * h + 1] = jnp.where(first, 0.0, q).T.astype(BF16)
        vt_ref[0, 0, h * 128:(h + 1) * 128, :] = v_ref[0, :, h * 128:(h + 1) * 128].astype(F32).T.astype(BF16)


def _attn_prep(p, cos_t, sin_t):
    nb, nt, _ = p.shape
    tm = ROW_TILE
    nti = nt // tm
    return pl.pallas_call(
        _aprep_kernel,
        grid=(nb, nti),
        in_specs=[pl.BlockSpec((1, tm, 2 * BR_W), lambda b, i: (b, i, 0)),
                  pl.BlockSpec((1, tm, BR_W), lambda b, i: (b, i, U_AV)),
                  pl.BlockSpec((tm, 128), lambda b, i: (i, 0)),
                  pl.BlockSpec((tm, 128), lambda b, i: (i, 0))],
        out_specs=[pl.BlockSpec((1, 2 * DA_HEADS, 128, tm), lambda b, i: (b, 0, 0, i)),
                   pl.BlockSpec((1, tm, BR_W), lambda b, i: (b, i, 0)),
                   pl.BlockSpec((1, 1, BR_W, tm), lambda b, i: (b, i, 0, 0))],
        out_shape=[jax.ShapeDtypeStruct((nb, 2 * DA_HEADS, 128, nt), BF16),
                   jax.ShapeDtypeStruct((nb, nt, BR_W), BF16),
                   jax.ShapeDtypeStruct((nb, nti, BR_W, tm), BF16)],
        compiler_params=_cparams("parallel", "parallel"),
        name="attn_prep",
    )(p, p, cos_t, sin_t)


def _lambda(lq_ref, lk_ref, lam_init):
    e = jnp.exp(jnp.sum(lq_ref[...] * lk_ref[...], axis=-1, keepdims=True))
    return e[0:1] - e[1:2] + lam_init


SUM_ROWS = 16


def _attn_kernel(lq_ref, lk_ref, g_ref, q1_ref, q2_ref, k_ref, vt_ref, z_ref, o_ref, acc1, acc2,
                 *, nkv, tk, lam_init):
    q1 = q1_ref[0, 0]
    q2 = q2_ref[0, 0]
    tq = q1.shape[1]
    acc1[...] = jnp.zeros_like(acc1)
    acc2[...] = jnp.zeros_like(acc2)
    ones = jnp.ones((SUM_ROWS, tk), BF16)

    def scores(q, j):
        s = jnp.dot(k_ref[0, j * tk:(j + 1) * tk, :], q, preferred_element_type=F32)
        return s, jnp.max(s, axis=0, keepdims=True)

    def accumulate(s_mx, acc, j, m):
        s, mx = s_mx
        vv = jnp.concatenate([vt_ref[0, j], ones], axis=0)
        mn = jnp.maximum(m, mx)
        p = jnp.exp2(s - mn).astype(BF16)
        acc[...] = acc[...] * jnp.exp2(m - mn) + jnp.dot(vv, p, preferred_element_type=F32)
        return mn

    m1 = m2 = jnp.full((1, tq), NEG, F32)
    cur1, cur2 = scores(q1, 0), scores(q2, 0)
    for j in range(nkv):
        if j + 1 < nkv:
            nxt1 = scores(q1, j + 1)
        m1 = accumulate(cur1, acc1, j, m1)
        if j + 1 < nkv:
            nxt2 = scores(q2, j + 1)
        m2 = accumulate(cur2, acc2, j, m2)
        cur1, cur2 = nxt1, nxt2

    lam = _lambda(lq_ref, lk_ref, lam_init)
    l1 = acc1[DA_DV:DA_DV + 1, :]
    l2 = acc2[DA_DV:DA_DV + 1, :]
    o = acc1[0:DA_DV, :] / l1 - lam * (acc2[0:DA_DV, :] / l2)
    ms = jnp.mean(o * o, axis=0, keepdims=True)
    y = (o * lax.rsqrt(ms + EPS)).T * g_ref[...] * (1.0 - lam_init)
    o_ref[0] = (y * _silu(z_ref[0].astype(F32))).astype(o_ref.dtype)


def _diff_attention(p, qt, kr, vt, lam_q, lam_k, da_g, lam_init, n_lat, tq):
    nb, nt, _ = p.shape
    nkv, tk = vt.shape[1], vt.shape[3]
    return pl.pallas_call(
        functools.partial(_attn_kernel, nkv=nkv, tk=tk, lam_init=lam_init),
        grid=(nb, DA_HEADS, n_lat // tq),
        in_specs=[pl.BlockSpec((2, DA_DH), lambda b, h, i: (0, 0)),
                  pl.BlockSpec((2, DA_DH), lambda b, h, i: (0, 0)),
                  pl.BlockSpec((1, DA_DV), lambda b, h, i: (0, 0)),
                  pl.BlockSpec((1, 1, 128, tq), lambda b, h, i: (b, 2 * h, 0, i)),
                  pl.BlockSpec((1, 1, 128, tq), lambda b, h, i: (b, 2 * h + 1, 0, i)),
                  pl.BlockSpec((1, nt, 128), lambda b, h, i: (b, 0, h)),
                  pl.BlockSpec((1, nkv, 128, tk), lambda b, h, i: (b, 0, h, 0)),
                  pl.BlockSpec((1, tq, 128), lambda b, h, i: (b, i, U_AZ * 4 + h))],
        out_specs=pl.BlockSpec((1, tq, 128), lambda b, h, i: (b, i, h)),
        out_shape=jax.ShapeDtypeStruct((nb, n_lat, BR_W), BF16),
        scratch_shapes=[pltpu.VMEM((DA_DV + SUM_ROWS, tq), F32), pltpu.VMEM((DA_DV + SUM_ROWS, tq), F32)],
        compiler_params=_cparams("parallel", "parallel", "parallel"),
        name="diff_attention",
    )(lam_q, lam_k, da_g.reshape(1, DA_DV), qt, qt, kr, vt, p)


HALO = 16


def _log_sigmoid(x):
    return jnp.minimum(x, 0.0) - jnp.log1p(jnp.exp(-jnp.abs(x)))


def _bprep_kernel(x_ref, prev_ref, next_ref, v_ref, g_ref, w_ref, cb_ref, gb_ref, k_ref, qt_ref, vt_ref, gl_ref,
                  glt_ref, *, n_lat, n_tok, tm):
    i = pl.program_id(1)
    x = x_ref[0].astype(F32)
    row = lax.broadcasted_iota(jnp.int32, (tm, 1), 0)
    pos = i * tm + row
    xp = jnp.where(row == 0, prev_ref[0, HALO - 1:HALO, :].astype(F32), pltpu.roll(x, 1, 0))
    xp = jnp.where((pos == 0) | (pos == n_lat), 0.0, xp)
    xn = jnp.where(row == tm - 1, next_ref[0, 0:1, :].astype(F32), pltpu.roll(x, tm - 1, 0))
    xn = jnp.where((pos == n_lat - 1) | (pos == n_tok - 1), 0.0, xn)
    y = xp * w_ref[0:1, :] + x * w_ref[1:2, :] + xn * w_ref[2:3, :] + cb_ref[...]
    y = _silu(y)
    k_ref[0] = (y[:, BR_W:] * (ML_DH ** -0.5)).astype(BF16)
    for c in range(tm // ML_CHUNK):
        rows = slice(c * ML_CHUNK, (c + 1) * ML_CHUNK)
        qt_ref[0, c] = y[rows, :BR_W].T.astype(BF16)
        vt_ref[0, c] = v_ref[0, rows, :].astype(F32).T.astype(BF16)
    g = g_ref[0] + gb_ref[...]
    gl_lane = lax.broadcasted_iota(jnp.int32, (1, 128), 1)
    is_f = (gl_lane % (2 * ML_HEADS)) >= ML_HEADS
    gl = jnp.where(is_f, _log_sigmoid(g), g)
    gl_ref[0] = gl
    for c in range(tm // ML_CHUNK):
        glt_ref[0, c] = gl[c * ML_CHUNK:(c + 1) * ML_CHUNK, :].T[:N_GATE, :]


def _mlstm_prep(p, gates, conv_w, conv_b, gate_b, n_lat):
    nb, nt, _ = p.shape
    tm = ROW_TILE
    hb = tm // HALO
    last = nt // HALO - 1
    gb = jnp.zeros((1, 128), F32).at[0, :N_GATE].set(gate_b.reshape(-1))
    cpt = tm // ML_CHUNK
    tspec = pl.BlockSpec((1, cpt, BR_W, ML_CHUNK), lambda b, i: (b, i, 0, 0))
    tshape = jax.ShapeDtypeStruct((nb, nt // ML_CHUNK, BR_W, ML_CHUNK), BF16)
    return pl.pallas_call(
        functools.partial(_bprep_kernel, n_lat=n_lat, n_tok=nt, tm=tm),
        grid=(nb, nt // tm),
        in_specs=[pl.BlockSpec((1, tm, 2 * BR_W), lambda b, i: (b, i, U_BQ // 2)),
                  pl.BlockSpec((1, HALO, 2 * BR_W), lambda b, i: (b, jnp.maximum(i * hb - 1, 0), U_BQ // 2)),
                  pl.BlockSpec((1, HALO, 2 * BR_W), lambda b, i: (b, jnp.minimum((i + 1) * hb, last), U_BQ // 2)),
                  pl.BlockSpec((1, tm, BR_W), lambda b, i: (b, i, U_BV)),
                  pl.BlockSpec((1, tm, 128), lambda b, i: (b, i, 0)),
                  pl.BlockSpec((3, 2 * BR_W), lambda b, i: (0, 0)),
                  pl.BlockSpec((1, 2 * BR_W), lambda b, i: (0, 0)),
                  pl.BlockSpec((1, 128), lambda b, i: (0, 0))],
        out_specs=[pl.BlockSpec((1, tm, BR_W), lambda b, i: (b, i, 0)), tspec, tspec,
                   pl.BlockSpec((1, tm, 128), lambda b, i: (b, i, 0)),
                   pl.BlockSpec((1, cpt, N_GATE, ML_CHUNK), lambda b, i: (b, i, 0, 0))],
        out_shape=[jax.ShapeDtypeStruct((nb, nt, BR_W), BF16), tshape, tshape,
                   jax.ShapeDtypeStruct((nb, nt, 128), F32),
                   jax.ShapeDtypeStruct((nb, nt // ML_CHUNK, N_GATE, ML_CHUNK), F32)],
        compiler_params=_cparams("parallel", "parallel"),
        name="mlstm_prep",
    )(p, p, p, p, gates, conv_w, conv_b.reshape(1, -1), gb)


def _split_dot(a, b_f32, a_is_tri):
    hi = b_f32.astype(BF16)
    lo = (b_f32 - hi.astype(F32)).astype(BF16)
    if a_is_tri:
        return jnp.dot(a, hi, preferred_element_type=F32) + jnp.dot(a, lo, preferred_element_type=F32)
    return jnp.dot(hi, a, preferred_element_type=F32) + jnp.dot(lo, a, preferred_element_type=F32)


def _scan_kernel(kf_ref, kb_ref, qtf_ref, qtb_ref, vtf_ref, vtb_ref, glf_ref, glb_ref, gtf_ref, gtb_ref,
                 hf_ref, hb_ref, c_scr, m_scr):
    t = pl.program_id(0)
    L = ML_CHUNK
    n_batch = kf_ref.shape[0]

    @pl.when(t == 0)
    def _():
        c_scr[...] = jnp.zeros_like(c_scr)
        m_scr[...] = jnp.zeros_like(m_scr)

    si = lax.broadcasted_iota(jnp.int32, (L, L), 0)
    ti = lax.broadcasted_iota(jnp.int32, (L, L), 1)
    ones_t = jnp.ones((SUM_ROWS, L), BF16)
    dirs = ((kf_ref, qtf_ref, vtf_ref, glf_ref, gtf_ref, hf_ref), (kb_ref, qtb_ref, vtb_ref, glb_ref, gtb_ref, hb_ref))
    chains = [(b, d, h) for b in range(n_batch) for d in range(2) for h in range(ML_HEADS)]
    hsl = lambda h: slice(h * ML_DH, (h + 1) * ML_DH)

    seen, gates = [], {}
    for d in range(2):
        sd = (si <= ti) if d == 0 else (si >= ti)
        tri_c = ((ti <= si) if d == 0 else (ti >= si)).astype(BF16)
        seen.append(sd)
        for b in range(n_batch):
            gl = dirs[d][3][b]
            gt = dirs[d][4][b, 0]
            gates[b, d] = (gl, _split_dot(tri_c, gl, True), _split_dot(sd.astype(BF16), gt, False))

    ks = [dirs[d][0][b, :, hsl(h)] for b, d, h in chains]
    qts = [dirs[d][1][b, 0, hsl(h), :] for b, d, h in chains]
    cmats = [c_scr[i] for i in range(len(chains))]
    raw = [jnp.dot(k, qt, preferred_element_type=F32) for k, qt in zip(ks, qts)]
    inter = [jnp.dot(c.astype(BF16), qt, preferred_element_type=F32) for c, qt in zip(cmats, qts)]

    vecs, h_loc, kv_loc = [], [], []
    for i, (b, d, h) in enumerate(chains):
        gl, bc_all, br_all = gates[b, d]
        ci = d * 2 * ML_HEADS + h
        cf = ci + ML_HEADS
        last = L - 1 if d == 0 else 0
        li_c = gl[:, ci:ci + 1]
        b_c = bc_all[:, cf:cf + 1]
        b_r = br_all[cf:cf + 1, :]
        b_tot = b_c[last:last + 1, :]
        dmat = jnp.where(seen[d], b_r + (li_c - b_c), NEG)
        rmax = jnp.max(dmat, axis=0, keepdims=True)
        s = (raw[i] * jnp.exp(dmat - rmax)).astype(BF16)
        g_c = b_tot - b_c + li_c
        g_max = jnp.max(g_c, axis=0, keepdims=True)
        wk = (ks[i].astype(F32) * jnp.exp(g_c - g_max)).astype(BF16)
        vt_aug = jnp.concatenate([dirs[d][2][b, 0, hsl(h), :], ones_t], axis=0)
        h_loc.append(jnp.dot(vt_aug, s, preferred_element_type=F32))
        kv_loc.append(jnp.dot(vt_aug, wk, preferred_element_type=F32))
        vecs.append((b_r, b_tot, rmax, g_max))

    for i, (b, d, h) in enumerate(chains):
        b_r, b_tot, rmax, g_max = vecs[i]
        m = m_scr[i][:, 0:1]
        m_new = jnp.maximum(b_tot + m, g_max)
        c_scr[i] = jnp.exp(b_tot + m - m_new) * cmats[i] + jnp.exp(g_max - m_new) * kv_loc[i]
        m_scr[i] = jnp.broadcast_to(m_new, (1, 128))
        m_inter = b_r + m
        m_t = jnp.maximum(m_inter, rmax)
        num = jnp.exp(m_inter - m_t) * inter[i] + jnp.exp(rmax - m_t) * h_loc[i]
        den = num[ML_DH:ML_DH + 1, :]
        dirs[d][5][b, 0, hsl(h), :] = num[:ML_DH, :] / jnp.maximum(jnp.abs(den), jnp.exp(-m_t))


def _mlstm_scan(kc, qt, vt, gl, glt, n_lat):
    nb, nt, _ = kc.shape
    L = ML_CHUNK
    nch = nt // L
    nlc = n_lat // L
    fwd = lambda t: (t + nlc) % nch
    bwd = lambda t: nch - 1 - t
    tspec = lambda f: pl.BlockSpec((nb, 1, BR_W, L), lambda t: (0, f(t), 0, 0))
    rspec = lambda width, f: pl.BlockSpec((nb, L, width), lambda t: (0, f(t), 0))
    gspec = lambda f: pl.BlockSpec((nb, 1, N_GATE, L), lambda t: (0, f(t), 0, 0))
    hshape = jax.ShapeDtypeStruct((nb, nch, BR_W, L), F32)
    n_chain = nb * 2 * ML_HEADS
    return pl.pallas_call(
        _scan_kernel,
        grid=(nch,),
        in_specs=[rspec(BR_W, fwd), rspec(BR_W, bwd), tspec(fwd), tspec(bwd), tspec(fwd), tspec(bwd),
                  rspec(128, fwd), rspec(128, bwd), gspec(fwd), gspec(bwd)],
        out_specs=[tspec(fwd), tspec(bwd)],
        out_shape=[hshape, hshape],
        scratch_shapes=[pltpu.VMEM((n_chain, ML_DH + SUM_ROWS, ML_DH), F32),
                        pltpu.VMEM((n_chain, 1, 128), F32)],
        compiler_params=_cparams("arbitrary"),
        name="mlstm_scan",
    )(kc, kc, qt, qt, vt, vt, gl, gl, glt, glt)


NA_RB = 4
NA_BLK = NA_RB * GRID_W
NA_SLAB = 3 * NA_RB


def _toeplitz_kernel(rpb_ref, sel_ref, ok_ref, o_ref):
    x = rpb_ref[...]
    hi = x.astype(BF16)
    r1 = x - hi.astype(F32)
    mid = r1.astype(BF16)
    lo = (r1 - mid.astype(F32)).astype(BF16)
    sel = sel_ref[...]
    y = (jnp.dot(hi, sel, preferred_element_type=F32) + jnp.dot(mid, sel, preferred_element_type=F32)
         + jnp.dot(lo, sel, preferred_element_type=F32))
    o_ref[...] = jnp.where(ok_ref[...] > 0.5, y, NEG)


def _na_bias_tables(rpb_all, rows):
    depth = rpb_all.shape[0]
    nrow, ncol = 2 * NA_KH - 1, 2 * NA_KW - 1
    c = np.arange(GRID_W)
    kcol = np.arange(GRID_W)
    cs = np.clip(c - NA_KW // 2, 0, GRID_W - NA_KW)
    ok = ((kcol[None, :] >= cs[:, None]) & (kcol[None, :] < cs[:, None] + NA_KW)).reshape(1, -1)
    cidx = np.clip(kcol[None, :] - c[:, None] + NA_KW - 1, 0, ncol - 1).reshape(-1)
    sel = np.zeros((128, GRID_W * GRID_W), np.float32)
    sel[cidx, np.arange(GRID_W * GRID_W)] = 1.0
    flat = jnp.pad(rpb_all.astype(F32).reshape(depth * NA_HEADS * nrow, ncol), ((0, 0), (0, 128 - ncol)))
    toep = pl.pallas_call(
        _toeplitz_kernel,
        out_shape=jax.ShapeDtypeStruct((flat.shape[0], GRID_W * GRID_W), F32),
        compiler_params=pltpu.CompilerParams(vmem_limit_bytes=VMEM_LIMIT),
        name="na_bias_toeplitz",
    )(flat, jnp.asarray(sel, BF16), jnp.asarray(ok, F32))
    toep = toep.reshape(depth, NA_HEADS, nrow, GRID_W, GRID_W)
    plan = []
    for r0, s0 in ((0, 0), (2 * NA_RB, NA_RB), (rows - NA_RB, rows - NA_SLAB)):
        for rr in range(NA_RB):
            r = r0 + rr
            rs = min(max(r - NA_KH // 2, 0), rows - NA_KH)
            plan.append(tuple(s0 + i - r + NA_KH - 1 if rs <= s0 + i < rs + NA_KH else -1 for i in range(NA_SLAB)))
    return pl.pallas_call(
        functools.partial(_bias_tile_kernel, plan=tuple(plan)),
        grid=(depth, NA_HEADS),
        in_specs=[pl.BlockSpec((1, 1, nrow, GRID_W, GRID_W), lambda l, h: (l, h, 0, 0, 0))],
        out_specs=pl.BlockSpec((1, 3, 1, NA_BLK, NA_SLAB * GRID_W), lambda l, h: (l, 0, h, 0, 0)),
        out_shape=jax.ShapeDtypeStruct((depth, 3, NA_HEADS, NA_BLK, NA_SLAB * GRID_W), F32),
        compiler_params=_cparams("parallel", "parallel"),
        name="na_bias_tiles",
    )(toep)


def _bias_tile_kernel(t_ref, o_ref, *, plan):
    masked = jnp.full((GRID_W, GRID_W), NEG, F32)
    for n, rows_shown in enumerate(plan):
        v, rr = divmod(n, NA_RB)
        blocks = [masked if r < 0 else t_ref[0, 0, r] for r in rows_shown]
        o_ref[0, v, 0, rr * GRID_W:(rr + 1) * GRID_W, :] = jnp.concatenate(blocks, axis=1)


def _na_kernel(q_ref, k0_ref, k1_ref, k2_ref, v0_ref, v1_ref, v2_ref, kx_ref, vx_ref, z_ref, bias_ref,
               o_ref, ks, vs):
    nb = NA_BLK
    ks[0:nb] = k0_ref[0]
    ks[nb:2 * nb] = k1_ref[0]
    ks[2 * nb:3 * nb] = k2_ref[0]
    vs[0:nb] = v0_ref[0]
    vs[nb:2 * nb] = v1_ref[0]
    vs[2 * nb:3 * nb] = v2_ref[0]
    first = lax.broadcasted_iota(jnp.int32, (1, 128), 1) < NA_DH
    nt_dims = (((1,), (1,)), ((), ()))
    outs = []
    for pr in range(NA_HEADS // 2):
        cols = slice(pr * 128, (pr + 1) * 128)
        qp = q_ref[0, :, cols].astype(F32) * (NA_DH ** -0.5)
        qq = jnp.concatenate([jnp.where(first, qp, 0.0), jnp.where(first, 0.0, qp)], axis=0).astype(BF16)
        bias = jnp.concatenate([bias_ref[0, 2 * pr], bias_ref[0, 2 * pr + 1]], axis=0)
        s_loc = lax.dot_general(qq, ks[:, cols], nt_dims, preferred_element_type=F32) + bias
        s_ctx = lax.dot_general(qq, kx_ref[0, :, cols], nt_dims, preferred_element_type=F32)
        m = jnp.maximum(jnp.max(s_loc, axis=-1, keepdims=True), jnp.max(s_ctx, axis=-1, keepdims=True))
        p_loc = jnp.exp(s_loc - m)
        p_ctx = jnp.exp(s_ctx - m)
        l = jnp.sum(p_loc, axis=-1, keepdims=True) + jnp.sum(p_ctx, axis=-1, keepdims=True)
        o = (jnp.dot(p_loc.astype(BF16), vs[:, cols], preferred_element_type=F32)
             + jnp.dot(p_ctx.astype(BF16), vx_ref[0, :, cols], preferred_element_type=F32)) / l
        outs.append(jnp.where(first, o[:nb], o[nb:]))
    o_all = jnp.concatenate(outs, axis=-1)
    o_ref[0] = (o_all * _silu(z_ref[0].astype(F32))).astype(o_ref.dtype)


def _neighbourhood_attention(p, bias_tab, layer, n_lat, n_ctx):
    nb, nt, _ = p.shape
    nblk = n_lat // NA_BLK
    blk = NA_BLK
    cblk = n_lat // n_ctx
    base = lambda i: jnp.clip(i - 1, 0, nblk - 3)
    variant = lambda i: jnp.where(i == 0, 0, jnp.where(i == nblk - 1, 2, 1))
    spec = lambda unit, off: pl.BlockSpec((1, blk, BR_W), lambda b, i: (b, base(i) + off, unit))
    own = lambda unit: pl.BlockSpec((1, blk, BR_W), lambda b, i: (b, i, unit))
    return pl.pallas_call(
        _na_kernel,
        grid=(nb, nblk),
        in_specs=[own(U_CQ),
                  spec(U_CK, 0), spec(U_CK, 1), spec(U_CK, 2),
                  spec(U_CV, 0), spec(U_CV, 1), spec(U_CV, 2),
                  pl.BlockSpec((1, n_ctx, BR_W), lambda b, i: (b, cblk, U_CK)),
                  pl.BlockSpec((1, n_ctx, BR_W), lambda b, i: (b, cblk, U_CV)),
                  own(U_CZ),
                  pl.BlockSpec((None, 1, NA_HEADS, blk, NA_SLAB * GRID_W),
                               lambda b, i: (layer, variant(i), 0, 0, 0))],
        out_specs=pl.BlockSpec((1, blk, BR_W), lambda b, i: (b, i, 0)),
        out_shape=jax.ShapeDtypeStruct((nb, n_lat, BR_W), BF16),
        scratch_shapes=[pltpu.VMEM((3 * blk, BR_W), BF16), pltpu.VMEM((3 * blk, BR_W), BF16)],
        compiler_params=_cparams("parallel", "parallel"),
        name="neighbourhood_attention",
    )(p, p, p, p, p, p, p, p, p, p, bias_tab)


def _ctx_kernel(lq_ref, lk_ref, g_ref, aqk_ref, avz_ref, cqk_ref, cvz_ref, ya_ref, yc_ref, *, lam_init):
    lam = _lambda(lq_ref, lk_ref, lam_init)
    first = lax.broadcasted_iota(jnp.int32, (1, 128), 1) < 64
    second = jnp.logical_not(first)
    nt_dims = (((1,), (1,)), ((), ()))

    def softmax(s):
        e = jnp.exp(s - jnp.max(s, axis=-1, keepdims=True))
        return e / jnp.sum(e, axis=-1, keepdims=True)

    def masked_scores(q, k, msk):
        return lax.dot_general(jnp.where(msk, q, 0.0).astype(BF16), k, nt_dims, preferred_element_type=F32)

    outs = []
    for h in range(DA_HEADS):
        cols = slice(h * 128, (h + 1) * 128)
        q = aqk_ref[0, :, cols].astype(F32) * (DA_DH ** -0.5)
        k = aqk_ref[0, :, BR_W + h * 128:BR_W + (h + 1) * 128]
        v = avz_ref[0, :, cols]
        z = avz_ref[0, :, BR_W + h * 128:BR_W + (h + 1) * 128].astype(F32)
        a = softmax(masked_scores(q, k, first)) - lam * softmax(masked_scores(q, k, second))
        o = jnp.dot(a.astype(BF16), v, preferred_element_type=F32)
        ms = jnp.mean(o * o, axis=-1, keepdims=True)
        y = o * lax.rsqrt(ms + EPS) * g_ref[...] * (1.0 - lam_init)
        outs.append(y * _silu(z))
    ya_ref[0] = jnp.concatenate(outs, axis=-1).astype(ya_ref.dtype)

    outs = []
    for pr in range(NA_HEADS // 2):
        cols = slice(pr * 128, (pr + 1) * 128)
        q = cqk_ref[0, :, cols].astype(F32) * (NA_DH ** -0.5)
        k = cqk_ref[0, :, BR_W + pr * 128:BR_W + (pr + 1) * 128]
        v = cvz_ref[0, :, cols]
        pair = [jnp.dot(softmax(masked_scores(q, k, msk)).astype(BF16), v, preferred_element_type=F32)
                for msk in (first, second)]
        outs.append(jnp.where(first, pair[0], pair[1]))
    z = cvz_ref[0, :, BR_W:].astype(F32)
    yc_ref[0] = (jnp.concatenate(outs, axis=-1) * _silu(z)).astype(yc_ref.dtype)


def _ctx_attention(p, lam_q, lam_k, da_g, lam_init, n_lat, n_ctx):
    nb = p.shape[0]
    cblk = n_lat // n_ctx
    pspec = lambda unit: pl.BlockSpec((1, n_ctx, 2 * BR_W), lambda b: (b, cblk, unit // 2))
    yspec = pl.BlockSpec((1, n_ctx, BR_W), lambda b: (b, 0, 0))
    small = lambda shape: pl.BlockSpec(shape, lambda b: (0, 0))
    yshape = jax.ShapeDtypeStruct((nb, n_ctx, BR_W), BF16)
    return pl.pallas_call(
        functools.partial(_ctx_kernel, lam_init=lam_init),
        grid=(nb,),
        in_specs=[small((2, DA_DH)), small((2, DA_DH)), small((1, DA_DV)),
                  pspec(U_AQ), pspec(U_AV), pspec(U_CQ), pspec(U_CV)],
        out_specs=[yspec, yspec],
        out_shape=[yshape, yshape],
        compiler_params=_cparams("parallel"),
        name="ctx_attention",
    )(lam_q, lam_k, da_g.reshape(1, DA_DV), p, p, p, p)


def _merge_kernel(x_ref, ya_ref, yc_ref, hf_ref, hb_ref, bo_ref, bz_ref, gm0_ref, gm1_ref, gm2_ref,
                  mod_ref, mlg_ref, fg_ref, wbr_ref, wout_ref, o_ref, *, n_batch, ctx_rows, final_norm):
    b = pl.program_id(0)
    hsum = jnp.concatenate([(hf_ref[0, c] + hb_ref[0, c]).T for c in range(hf_ref.shape[1])], axis=0)
    parts = []
    for h in range(ML_HEADS):
        hh = hsum[:, h * ML_DH:(h + 1) * ML_DH]
        ms = jnp.mean(hh * hh, axis=-1, keepdims=True)
        parts.append(hh * lax.rsqrt(ms + EPS) * mlg_ref[...])
    hn = jnp.concatenate(parts, axis=-1)
    yb = (jax.nn.sigmoid(bo_ref[0].astype(F32)) * hn * _silu(bz_ref[0].astype(F32))).astype(BF16)

    def gated(gm_ref, y, n):
        return jax.nn.sigmoid(gm_ref[0].astype(F32)) * jnp.dot(y, wbr_ref[n], preferred_element_type=F32)

    merged = gated(gm0_ref, ya_ref[0], 0) + gated(gm1_ref, yb, 1) + gated(gm2_ref, yc_ref[0], 2)
    upd = jnp.dot(merged.astype(BF16), wout_ref[...], preferred_element_type=F32)

    gate = mod_ref[n_batch:n_batch + 1, 2 * D_MODEL:] if ctx_rows else mod_ref[pl.ds(b, 1), 2 * D_MODEL:]
    xo = x_ref[0] + gate * upd
    if final_norm:
        ms = jnp.mean(xo * xo, axis=-1, keepdims=True)
        xo = xo * lax.rsqrt(ms + EPS) * fg_ref[...]
    o_ref[0] = xo


def _merge(xa, p, ya, yc, hf, hb, mod, ml_g, final_g, wbr, wout, n_rows, tm, row_off, ctx_rows, final_norm):
    nb, _, d = xa.shape
    own = lambda width: pl.BlockSpec((1, tm, width), lambda b, i: (b, i, 0))
    row = lambda width, unit: pl.BlockSpec((1, tm, width), lambda b, i: (b, i + row_off, unit))
    hspec = pl.BlockSpec((1, tm // ML_CHUNK, BR_W, ML_CHUNK), lambda b, i: (b, i + row_off, 0, 0))
    const = lambda shape: pl.BlockSpec(shape, lambda b, i: (0,) * len(shape))
    return pl.pallas_call(
        functools.partial(_merge_kernel, n_batch=nb, ctx_rows=ctx_rows, final_norm=final_norm),
        grid=(nb, n_rows // tm),
        in_specs=[row(d, 0), own(BR_W), own(BR_W), hspec, hspec,
                  row(BR_W, U_BO), row(BR_W, U_BZ),
                  row(d, U_GM // 2), row(d, U_GM // 2 + 1), row(d, U_GM // 2 + 2),
                  const((8, 3 * d)), const((1, ML_DH)), const((1, d)),
                  const((3, BR_W, d)), const((d, d))],
        out_specs=own(d),
        out_shape=jax.ShapeDtypeStruct((nb, n_rows, d), F32),
        compiler_params=_cparams("parallel", "parallel"),
        name="merge",
    )(xa, ya, yc, hf, hb, p, p, p, p, p, mod, ml_g.reshape(1, ML_DH), final_g.reshape(1, d), wbr, wout)


def _rope_tables(n_lat, n_tok):
    t = jnp.arange(n_lat, dtype=jnp.int32)
    row = (t // GRID_W).astype(F32)
    col = (t % GRID_W).astype(F32)
    inv = ROPE_BASE ** (-jnp.arange(0, DA_DH // 2, 2, dtype=F32) / (DA_DH // 2))
    ang = jnp.concatenate([row[:, None] * inv, col[:, None] * inv], axis=-1)
    cos = jnp.repeat(jnp.cos(ang), 2, axis=-1)
    sin = jnp.repeat(jnp.sin(ang), 2, axis=-1) * jnp.tile(jnp.asarray([-1.0, 1.0], F32), DA_DH // 2)
    cos = jnp.tile(cos, (1, 2))
    sin = jnp.tile(sin, (1, 2))
    cos = jnp.concatenate([cos, jnp.ones((n_tok - n_lat, 128), F32)], axis=0)
    sin = jnp.concatenate([sin, jnp.zeros((n_tok - n_lat, 128), F32)], axis=0)
    return cos, sin


def _w_layout_kernel(w_ref, o_ref):
    split = 9 * BR_W + N_GATE
    resume = split + GATE_PAD - N_GATE
    rows = w_ref.shape[1]
    o_ref[0, :, :split] = w_ref[0, :, :split].astype(BF16)
    o_ref[0, :, split:resume] = jnp.zeros((rows, resume - split), BF16)
    o_ref[0, :, resume:] = w_ref[0, :, split:].astype(BF16)


def _pad_w_in(w):
    depth, d, n_in = w.shape
    tr = 256
    return pl.pallas_call(
        _w_layout_kernel,
        grid=(depth, d // tr),
        in_specs=[pl.BlockSpec((1, tr, n_in), lambda l, i: (l, i, 0))],
        out_specs=pl.BlockSpec((1, tr, P_W), lambda l, i: (l, i, 0)),
        out_shape=jax.ShapeDtypeStruct((depth, d, P_W), BF16),
        compiler_params=_cparams("parallel", "parallel"),
        name="w_in_layout",
    )(w)


def kernel(x, c, ctx, c_ctx, w_mod, b_mod, norm_g, w_in, da_lam_q, da_lam_k, da_norm_g, ml_conv_w, ml_conv_b,
           ml_gate_b, ml_norm_g, na_rpb, w_br, w_out, final_g):
    nb, n_lat, d = x.shape
    n_ctx = ctx.shape[1]
    nt = n_lat + n_ctx
    depth = w_mod.shape[0]
    assert d == D_MODEL and nt % ROW_TILE == 0 and n_lat % n_ctx == 0 and n_lat % NA_BLK == 0 and n_lat >= 4 * NA_BLK and nb + 1 <= 8
    tq = 2048

    cs =jnp.concatenate([c, c_ctx[None], jnp.zeros((8 - nb - 1, d), F32)], axis=0)
    mods = _modulation(cs, w_mod, b_mod)
    cos_t, sin_t = _rope_tables(n_lat, nt)
    xa = jnp.concatenate([x, ctx], axis=1)
    w_in_p = _pad_w_in(w_in)
    w_br_b = w_br.astype(BF16)
    w_out_b = w_out.astype(BF16)
    bias_tabs = _na_bias_tables(na_rpb, n_lat // GRID_W)

    for l in range(depth):
        last = l == depth - 1
        lam_init = 0.8 - 0.6 * math.exp(-0.3 * l)
        p, gates = _projection(xa, mods[l], norm_g[l], w_in_p, l, n_lat)

        qt, kr, vt = _attn_prep(p, cos_t, sin_t)
        ya = _diff_attention(p, qt, kr, vt, da_lam_q[l], da_lam_k[l], da_norm_g[l], lam_init, n_lat, tq)

        kc, qtc, vtc, gl, glt = _mlstm_prep(p, gates, ml_conv_w[l], ml_conv_b[l], ml_gate_b[l], n_lat)
        hf, hb = _mlstm_scan(kc, qtc, vtc, gl, glt, n_lat)

        yc = _neighbourhood_attention(p, bias_tabs, l, n_lat, n_ctx)

        merge = functools.partial(_merge, xa, p, hf=hf, hb=hb, mod=mods[l], ml_g=ml_norm_g[l], final_g=final_g,
                                  wbr=w_br_b[l], wout=w_out_b[l])
        x_lat = merge(ya=ya, yc=yc, n_rows=n_lat, tm=512, row_off=0, ctx_rows=False, final_norm=last)
        if last:
            return x_lat
        ya_c, yc_c = _ctx_attention(p, da_lam_q[l], da_lam_k[l], da_norm_g[l], lam_init, n_lat, n_ctx)
        x_ctx = merge(ya=ya_c, yc=yc_c, n_rows=n_ctx, tm=n_ctx, row_off=n_lat // n_ctx, ctx_rows=True,
                      final_norm=False)
        xa = jnp.concatenate([x_lat, x_ctx], axis=1)
```
